```python
import jax
import jax.numpy as jnp
from jax import lax
import numpy as np


D_MODEL = 1024
BATCH = 4
SEQ = 8192
DEPTH = 2
DEC_BATCH = 32
DEC_SEQ = 4
PAST_LEN = 16384
PAGE_SIZE = 128

N_A_LAYERS = DEPTH // 2
N_B_LAYERS = DEPTH - N_A_LAYERS
GLA_HEADS = 4
GLA_DK = D_MODEL // 2 // GLA_HEADS
GLA_DV = D_MODEL // GLA_HEADS
GLA_KEY_DIM = GLA_HEADS * GLA_DK
GLA_VAL_DIM = GLA_HEADS * GLA_DV
GLA_GATE_RANK = 16
GLA_GATE_NORM = 16.0
GLA_CHUNK = 64
MOBA_HEADS = 16
MOBA_HD = D_MODEL // MOBA_HEADS
MOBA_DIM = MOBA_HEADS * MOBA_HD
MOBA_BLOCK = 256
MOBA_TOPK = 3
MOBA_Q_CHUNK = 16
ROPE_THETA = 10000.0
D_FF = 2816
CONV_W = 3
DN_ALPHA = (2 * DEPTH) ** 0.25
DN_BETA = (8 * DEPTH) ** -0.25
LN_EPS = 1e-5

kernel_name = 'yoco_gla_moba_convffn_decoder_step'


def layer_norm(x, g, b):
    xf = x.astype(jnp.float32)
    mu = jnp.mean(xf, axis=-1, keepdims=True)
    var = jnp.mean(jnp.square(xf - mu), axis=-1, keepdims=True)
    return ((xf - mu) * lax.rsqrt(var + LN_EPS) * g + b).astype(x.dtype)


def rms_norm(x, g):
    xf = x.astype(jnp.float32)
    return xf * lax.rsqrt(jnp.mean(xf * xf, axis=-1, keepdims=True) + LN_EPS) * g


def rope(x, pos):
    half = x.shape[-1] // 2
    inv = ROPE_THETA ** (-jnp.arange(half, dtype=jnp.float32) / half)
    ang = pos.astype(jnp.float32)[:, None] * inv[None, :]
    cos = jnp.cos(ang)[:, None, :]
    sin = jnp.sin(ang)[:, None, :]
    xf = x.astype(jnp.float32)
    x1, x2 = xf[..., :half], xf[..., half:]
    return jnp.concatenate([x1 * cos - x2 * sin, x1 * sin + x2 * cos], axis=-1).astype(x.dtype)


def gla_chunked(q, k, v, gk, s0):
    b_, t, h, _ = q.shape
    dv = v.shape[-1]
    c = min(GLA_CHUNK, t)
    pad = (-t) % c
    n = (t + pad) // c

    def to_chunks(a):
        a = jnp.pad(a.astype(jnp.float32), ((0, 0), (0, pad), (0, 0), (0, 0)))
        return a.reshape(b_, n, c, h, a.shape[-1]).transpose(1, 0, 3, 2, 4)

    qc, kc, vc, gc = to_chunks(q), to_chunks(k), to_chunks(v), to_chunks(gk)
    cum = jnp.cumsum(gc, axis=3)
    last = cum[:, :, :, -1:, :]
    q_in = qc * jnp.exp(cum)
    k_in = kc * jnp.exp(-cum)
    k_end = kc * jnp.exp(last - cum)
    mask = jnp.tril(jnp.ones((c, c), jnp.float32))

    def step(s, inp):
        qi, ki, ke, vi, la = inp
        a = jnp.einsum('bhcd,bhsd->bhcs', qi, ki) * mask
        o = jnp.einsum('bhcs,bhsv->bhcv', a, vi) + jnp.einsum('bhcd,bhdv->bhcv', qi, s)
        s = s * jnp.exp(la)[:, :, 0, :, None] + jnp.einsum('bhsd,bhsv->bhdv', ke, vi)
        return s, o

    s_fin, o = lax.scan(step, s0.astype(jnp.float32), (q_in, k_in, k_end, vc, last))
    o = o.transpose(1, 0, 3, 2, 4).reshape(b_, n * c, h, dv)[:, :t]
    return o, s_fin


def gla_mixer(h, s0, w_in, w_gk1, w_gk2, b_gk, norm_g, w_out):
    b_, t, _ = h.shape
    q, k, v, g = jnp.split(h @ w_in, [GLA_KEY_DIM, 2 * GLA_KEY_DIM, 2 * GLA_KEY_DIM + GLA_VAL_DIM], axis=-1)
    shp_k = (b_, t, GLA_HEADS, GLA_DK)
    shp_v = (b_, t, GLA_HEADS, GLA_DV)
    gk = jax.nn.log_sigmoid(((h @ w_gk1) @ w_gk2 + b_gk).astype(jnp.float32)) / GLA_GATE_NORM
    o, s = gla_chunked(q.reshape(shp_k) * (GLA_DK ** -0.5), k.reshape(shp_k), v.reshape(shp_v),
                       gk.reshape(shp_k), s0)
    o = rms_norm(o, norm_g) * jax.nn.silu(g.reshape(shp_v).astype(jnp.float32))
    return o.reshape(b_, t, GLA_VAL_DIM).astype(h.dtype) @ w_out, s.astype(s0.dtype)


def conv_ffn(h, state, w_up, w_conv, b_conv, w_down):
    u, g = jnp.split(h @ w_up, 2, axis=-1)
    t = h.shape[1]
    ext = jnp.concatenate([state.astype(g.dtype), g], axis=1)
    gc = b_conv + sum(ext[:, j:j + t] * w_conv[j] for j in range(CONV_W))
    a = jax.nn.gelu(gc, approximate=False) * u
    return a @ w_down, ext[:, t:]


def moba_attend(q, k, v, q_pos):
    tq, n_h, hd = q.shape
    tk = k.shape[0]
    nb = -(-tk // MOBA_BLOCK)
    pad_k = nb * MOBA_BLOCK - tk
    kb = jnp.pad(k, ((0, pad_k), (0, 0), (0, 0))).reshape(nb, MOBA_BLOCK, n_h, hd).transpose(2, 0, 1, 3)
    vb = jnp.pad(v, ((0, pad_k), (0, 0), (0, 0))).reshape(nb, MOBA_BLOCK, n_h, hd).transpose(2, 0, 1, 3)
    k_mean = jnp.mean(kb.astype(jnp.float32), axis=2)
    n_sel = min(MOBA_TOPK, nb)
    qc = min(MOBA_Q_CHUNK, tq)
    nq = -(-tq // qc)
    pad_q = nq * qc - tq
    qp = jnp.pad(q, ((0, pad_q), (0, 0), (0, 0))).reshape(nq, qc, n_h, hd)
    pp = jnp.pad(q_pos, (0, pad_q)).reshape(nq, qc)
    blk_ar = jnp.arange(nb)
    row_ar = jnp.arange(MOBA_BLOCK)
    head_ar = jnp.arange(n_h)[:, None, None]
    scale = hd ** -0.5

    def one_chunk(args):
        qq, pos = args
        own = pos // MOBA_BLOCK
        gate = jnp.einsum('qhd,hnd->hqn', qq.astype(jnp.float32), k_mean)
        gate = jnp.where(blk_ar[None, None, :] < own[None, :, None], gate, -jnp.inf)
        _, top = lax.top_k(gate, n_sel)
        idx = jnp.concatenate([top, jnp.broadcast_to(own[None, :, None], (n_h, qc, 1))], axis=-1)
        ks = kb[head_ar, idx]
        vs = vb[head_ar, idx]
        s = jnp.einsum('qhd,hqjkd->hqjk', qq, ks).astype(jnp.float32) * scale
        sel_ok = jnp.broadcast_to((top < own[None, :, None])[..., None], (n_h, qc, n_sel, MOBA_BLOCK))
        own_ok = (own[:, None] * MOBA_BLOCK + row_ar[None, :]) <= pos[:, None]
        own_ok = jnp.broadcast_to(own_ok[None, :, None, :], (n_h, qc, 1, MOBA_BLOCK))
        ok = jnp.concatenate([sel_ok, own_ok], axis=2)
        s = jnp.where(ok, s, -jnp.inf).reshape(n_h, qc, (n_sel + 1) * MOBA_BLOCK)
        p = jax.nn.softmax(s, axis=-1).reshape(n_h, qc, n_sel + 1, MOBA_BLOCK).astype(vs.dtype)
        return jnp.einsum('hqjk,hqjkd->qhd', p, vs)

    o = lax.map(one_chunk, (qp, pp))
    return o.reshape(nq * qc, n_h, hd)[:tq]


def moba_mixer(h, pos, w_q, w_out, k, v, attend):
    b_, t, _ = h.shape
    q = rope((h @ w_q).reshape(b_, t, MOBA_HEADS, MOBA_HD), pos)
    return attend(q, k, v).reshape(b_, t, MOBA_DIM) @ w_out


def shared_kv(x, silu_c, pos, kv_w_ada, kv_b_ada, kv_w):
    b_, t, _ = x.shape
    sh, sc = jnp.split((silu_c @ kv_w_ada + kv_b_ada)[:, None, :], 2, axis=-1)
    h = x * (1 + sc) + sh
    k, v = jnp.split(h @ kv_w, 2, axis=-1)
    k = rope(k.reshape(b_, t, MOBA_HEADS, MOBA_HD), pos)
    return k, v.reshape(b_, t, MOBA_HEADS, MOBA_HD)


def run_trunk(x, c, pos, gla_s0, conv_s0, attend, p):
    silu_c = jax.nn.silu(c)
    gla_states, conv_states = [], []
    k_sh, v_sh = None, None
    for i in range(DEPTH):
        mod = (silu_c @ p['w_ada'][i] + p['b_ada'][i])[:, None, :]
        sh_m, sc_m, g_m, sh_f, sc_f, g_f = jnp.split(mod, 6, axis=-1)
        h = x * (1 + sc_m) + sh_m
        if i < N_A_LAYERS:
            y, s = gla_mixer(h, gla_s0[i], p['a_w_in'][i], p['a_w_gk1'][i], p['a_w_gk2'][i],
                             p['a_b_gk'][i], p['a_norm_g'][i], p['a_w_out'][i])
            gla_states.append(s)
        else:
            j = i - N_A_LAYERS
            y = moba_mixer(h, pos, p['b_w_q'][j], p['b_w_out'][j], k_sh, v_sh, attend)
        x = layer_norm(DN_ALPHA * x + (1 + g_m) * y, p['ln_g'][i, 0], p['ln_b'][i, 0])
        h = x * (1 + sc_f) + sh_f
        y, cs = conv_ffn(h, conv_s0[i], p['w_up'][i], p['w_conv'][i], p['b_conv'][i], p['w_down'][i])
        conv_states.append(cs)
        x = layer_norm(DN_ALPHA * x + (1 + g_f) * y, p['ln_g'][i, 1], p['ln_b'][i, 1])
        if i == N_A_LAYERS - 1:
            k_sh, v_sh = shared_kv(x, silu_c, pos, p['kv_w_ada'], p['kv_b_ada'], p['kv_w'])
    return x, k_sh, v_sh, jnp.stack(gla_states), jnp.stack(conv_states)


def setup_inputs(seed: int = 0) -> dict:
    key = jax.random.key(seed)
    ks = jax.random.split(key, 32)
    d = D_MODEL

    def nrm(i, shape, scale):
        return jax.random.normal(ks[i], shape, jnp.float32) * scale

    n_pages = PAST_LEN // PAGE_SIZE
    n_used = DEC_BATCH * n_pages
    n_phys = n_used + n_used // 4
    page_table = jax.random.permutation(ks[6], n_phys)[:n_used].reshape(DEC_BATCH, n_pages).astype(jnp.int32)
    a_col_scale = jnp.concatenate([jnp.ones((2 * GLA_KEY_DIM,), jnp.float32),
                                   jnp.full((GLA_VAL_DIM,), DN_BETA, jnp.float32),
                                   jnp.ones((GLA_VAL_DIM,), jnp.float32)])
    kv_col_scale = jnp.concatenate([jnp.ones((MOBA_DIM,), jnp.float32),
                                    jnp.full((MOBA_DIM,), DN_BETA, jnp.float32)])
    return {
        'x_prompt': nrm(0, (BATCH, SEQ, d), 1.0),
        'x_sample': nrm(1, (DEC_BATCH, DEC_SEQ, d), 1.0),
        'cache_k': nrm(2, (n_phys, PAGE_SIZE, MOBA_HEADS, MOBA_HD), 1.0),
        'cache_v': nrm(3, (n_phys, PAGE_SIZE, MOBA_HEADS, MOBA_HD), DN_BETA),
        'state_gla': nrm(4, (N_A_LAYERS, DEC_BATCH, GLA_HEADS, GLA_DK, GLA_DV), 1.0),
        'state_ffn_conv': nrm(5, (DEPTH, DEC_BATCH, CONV_W - 1, D_FF), 1.0),
        'page_table': page_table,
        'c_prompt': nrm(7, (BATCH, d), 1.0),
        'c_sample': nrm(8, (DEC_BATCH, d), 1.0),
        'w_ada': nrm(9, (DEPTH, d, 6 * d), 0.1 * d ** -0.5),
        'b_ada': nrm(10, (DEPTH, 6 * d), 0.01),
        'ln_g': 1.0 + nrm(11, (DEPTH, 2, d), 0.02),
        'ln_b': nrm(12, (DEPTH, 2, d), 0.02),
        'w_up': nrm(13, (DEPTH, d, 2 * D_FF), d ** -0.5),
        'w_conv': nrm(14, (DEPTH, CONV_W, D_FF), CONV_W ** -0.5),
        'b_conv': nrm(15, (DEPTH, D_FF), 0.02),
        'w_down': nrm(16, (DEPTH, D_FF, d), DN_BETA * D_FF ** -0.5),
        'a_w_in': nrm(17, (N_A_LAYERS, d, 2 * GLA_KEY_DIM + 2 * GLA_VAL_DIM), d ** -0.5) * a_col_scale,
        'a_w_gk1': nrm(18, (N_A_LAYERS, d, GLA_GATE_RANK), d ** -0.5),
        'a_w_gk2': nrm(19, (N_A_LAYERS, GLA_GATE_RANK, GLA_KEY_DIM), GLA_GATE_RANK ** -0.5),
        'a_b_gk': nrm(20, (N_A_LAYERS, GLA_KEY_DIM), 0.1),
        'a_norm_g': 1.0 + nrm(21, (N_A_LAYERS, GLA_DV), 0.02),
        'a_w_out': nrm(22, (N_A_LAYERS, GLA_VAL_DIM, d), DN_BETA * GLA_VAL_DIM ** -0.5),
        'kv_w_ada': nrm(23, (d, 2 * d), 0.1 * d ** -0.5),
        'kv_b_ada': nrm(24, (2 * d,), 0.01),
        'kv_w': nrm(25, (d, 2 * MOBA_DIM), d ** -0.5) * kv_col_scale,
        'b_w_q': nrm(26, (N_B_LAYERS, d, MOBA_DIM), d ** -0.5),
        'b_w_out': nrm(27, (N_B_LAYERS, MOBA_DIM, d), DN_BETA * MOBA_DIM ** -0.5),
    }


def reference(x_prompt, x_sample, cache_k, cache_v, state_gla, state_ffn_conv, page_table,
              c_prompt, c_sample, w_ada, b_ada, ln_g, ln_b, w_up, w_conv, b_conv, w_down,
              a_w_in, a_w_gk1, a_w_gk2, a_b_gk, a_norm_g, a_w_out, kv_w_ada, kv_b_ada, kv_w,
              b_w_q, b_w_out):
    p = {'w_ada': w_ada, 'b_ada': b_ada, 'ln_g': ln_g, 'ln_b': ln_b, 'w_up': w_up,
         'w_conv': w_conv, 'b_conv': b_conv, 'w_down': w_down, 'a_w_in': a_w_in,
         'a_w_gk1': a_w_gk1, 'a_w_gk2': a_w_gk2, 'a_b_gk': a_b_gk, 'a_norm_g': a_norm_g,
         'a_w_out': a_w_out, 'kv_w_ada': kv_w_ada, 'kv_b_ada': kv_b_ada, 'kv_w': kv_w,
         'b_w_q': b_w_q, 'b_w_out': b_w_out}
    b_p, t_p = x_prompt.shape[0], x_prompt.shape[1]
    t_s = x_sample.shape[1]
    past_len = page_table.shape[1] * cache_k.shape[1]
    pos_p = jnp.arange(t_p, dtype=jnp.int32)
    pos_s = past_len + jnp.arange(t_s, dtype=jnp.int32)

    def attend_prompt(q, k, v):
        return jax.vmap(moba_attend, in_axes=(0, 0, 0, None))(q, k, v, pos_p)

    def attend_sample(q, k, v):
        def one_seq(args):
            pages, qb, kn, vn = args
            k_all = jnp.concatenate([cache_k[pages].reshape(past_len, MOBA_HEADS, MOBA_HD), kn], axis=0)
            v_all = jnp.concatenate([cache_v[pages].reshape(past_len, MOBA_HEADS, MOBA_HD), vn], axis=0)
            return moba_attend(qb, k_all, v_all, pos_s)
        return lax.map(one_seq, (page_table, q, k, v))

    gla0 = jnp.zeros((N_A_LAYERS, b_p, GLA_HEADS, GLA_DK, GLA_DV), x_prompt.dtype)
    conv0 = jnp.zeros((DEPTH, b_p, CONV_W - 1, D_FF), x_prompt.dtype)
    y_prompt, k_prompt, v_prompt, gla_prompt, conv_prompt = run_trunk(
        x_prompt, c_prompt, pos_p, gla0, conv0, attend_prompt, p)
    y_sample, k_sample, v_sample, gla_sample, conv_sample = run_trunk(
        x_sample, c_sample, pos_s, state_gla, state_ffn_conv, attend_sample, p)
    return (y_prompt, y_sample, k_prompt, v_prompt, k_sample, v_sample,
            gla_prompt, gla_sample, conv_prompt, conv_sample)
```

```python
import functools

import jax
import jax.numpy as jnp
from jax import lax
from jax.experimental import pallas as pl
from jax.experimental.pallas import tpu as pltpu

F32 = jnp.float32
BF = jnp.bfloat16

LN_EPS = 1e-5
GLA_CHUNK = 64
GLA_GATE_NORM = 16.0
GLA_GATE_RANK_PAD = 128
MOBA_BLOCK = 256
MOBA_TOPK = 3
ROPE_THETA = 10000.0
MASKED = -1e30
LANES = 128
VMEM_LIMIT = 56 * 1024 * 1024


def _params(*sem):
    return pltpu.CompilerParams(dimension_semantics=sem, vmem_limit_bytes=VMEM_LIMIT)


def _dot(a, b):
    return jnp.dot(a, b, preferred_element_type=F32)


def _dot_nt(a, b):
    return lax.dot_general(a, b, (((1,), (1,)), ((), ())), preferred_element_type=F32)


def _dot_tn(a, b):
    return lax.dot_general(a, b, (((0,), (0,)), ((), ())), preferred_element_type=F32)


def _layer_norm(z, g, b):
    mu = jnp.mean(z, axis=-1, keepdims=True)
    zc = z - mu
    var = jnp.mean(zc * zc, axis=-1, keepdims=True)
    return zc * lax.rsqrt(var + LN_EPS) * g + b


def _const_spec(shape):
    nd = len(shape)
    return pl.BlockSpec(shape, lambda *_: (0,) * nd)


def _mod_spec(arr, tm):
    if arr.shape[1] == 1:
        return pl.BlockSpec((1, 1, arr.shape[2]), lambda b, t: (b, 0, 0))
    return pl.BlockSpec((1, tm, arr.shape[2]), lambda b, t: (b, t, 0))


def _row_spec(tm, width):
    return pl.BlockSpec((1, tm, width), lambda b, t: (b, t, 0))


def _ada_kernel(c_ref, w_ref, b_ref, o_ref):
    c = c_ref[...]
    s = c * jax.nn.sigmoid(c)
    o_ref[...] = _dot(s.astype(BF), w_ref[...].astype(BF)) + b_ref[...]


def _ada(c, w3, layer, b2, tn=1024):
    r, d = c.shape
    n = w3.shape[2]
    return pl.pallas_call(
        _ada_kernel,
        out_shape=jax.ShapeDtypeStruct((r, n), F32),
        grid=(n // tn,),
        in_specs=[pl.BlockSpec((r, d), lambda j: (0, 0)),
                  pl.BlockSpec((None, d, tn), lambda j: (layer, 0, j)),
                  pl.BlockSpec((1, tn), lambda j: (0, j))],
        out_specs=pl.BlockSpec((r, tn), lambda j: (0, j)),
        compiler_params=_params("arbitrary"),
        name="ada_mod",
    )(c, w3, b2)


def _gla_proj_kernel(x_ref, sh_ref, sc_ref, win_ref, wg1_ref, wg2_ref, bgk_ref,
                     q_ref, k_ref, v_ref, g_ref, gk_ref, *, kd, vd, scale):
    h = (x_ref[0] * (1.0 + sc_ref[0]) + sh_ref[0]).astype(BF)
    q_ref[0] = _dot(h, win_ref[:, 0:kd]) * scale
    k_ref[0] = _dot(h, win_ref[:, kd:2 * kd])
    v_ref[0] = _dot(h, win_ref[:, 2 * kd:2 * kd + vd]).astype(BF)
    g_ref[0] = _dot(h, win_ref[:, 2 * kd + vd:2 * kd + 2 * vd])
    r = _dot(h, wg1_ref[...])
    lin = _dot(r.astype(BF), wg2_ref[...]) + bgk_ref[...]
    gk_ref[0] = (jnp.minimum(lin, 0.0) - jnp.log1p(jnp.exp(-jnp.abs(lin)))) * (1.0 / GLA_GATE_NORM)


def _gla_proj(x, sh, sc, w_in, wg1, wg2, bgk, kd, vd, scale, tm):
    b, t, d = x.shape
    kern = functools.partial(_gla_proj_kernel, kd=kd, vd=vd, scale=scale)
    return pl.pallas_call(
        kern,
        out_shape=[jax.ShapeDtypeStruct((b, t, kd), F32), jax.ShapeDtypeStruct((b, t, kd), F32),
                   jax.ShapeDtypeStruct((b, t, vd), BF), jax.ShapeDtypeStruct((b, t, vd), F32),
                   jax.ShapeDtypeStruct((b, t, kd), F32)],
        grid=(b, t // tm),
        in_specs=[_row_spec(tm, d), _mod_spec(sh, tm), _mod_spec(sc, tm),
                  _const_spec(w_in.shape), _const_spec(wg1.shape), _const_spec(wg2.shape),
                  _const_spec(bgk.shape)],
        out_specs=[_row_spec(tm, kd), _row_spec(tm, kd), _row_spec(tm, vd), _row_spec(tm, vd),
                   _row_spec(tm, kd)],
        compiler_params=_params("parallel", "parallel"),
        name="gla_proj",
    )(x, sh, sc, w_in, wg1, wg2, bgk)


def _split3(x):
    hi = x.astype(BF)
    r = x - hi.astype(F32)
    mid = r.astype(BF)
    lo = (r - mid.astype(F32)).astype(BF)
    return hi, mid, lo


def _gla_rec_kernel(q_ref, k_ref, v_ref, g_ref, gk_ref, s0_ref, ng_ref, o_ref, sfin_ref, s_scr,
                    *, heads, dk, dv, chunk, n_chunks):
    t = pl.program_id(1)

    @pl.when(t == 0)
    def _():
        s_scr[...] = s0_ref[0]

    row = lax.broadcasted_iota(jnp.int32, (chunk, chunk), 0)
    col = lax.broadcasted_iota(jnp.int32, (chunk, chunk), 1)
    tril_f = (row >= col).astype(F32)
    tril = tril_f.astype(BF)
    ng = ng_ref[...]

    for c in range(n_chunks):
        rows = pl.ds(c * chunk, chunk)
        hi, mid, lo = _split3(gk_ref[0, rows, :])
        cum = _dot(tril, hi) + _dot(tril, mid) + _dot(tril, lo)
        last = cum[chunk - 1:chunk, :]
        q_in = (q_ref[0, rows, :] * jnp.exp(cum)).astype(BF)
        kc = k_ref[0, rows, :]
        k_in = (kc * jnp.exp(-cum)).astype(BF)
        k_end = (kc * jnp.exp(last - cum)).astype(BF)
        dec = jnp.exp(last)
        for h in range(heads):
            ks = slice(h * dk, (h + 1) * dk)
            vs = slice(h * dv, (h + 1) * dv)
            a = _dot_nt(q_in[:, ks], k_in[:, ks]) * tril_f
            vh = v_ref[0, rows, vs]
            st = s_scr[h]
            o = _dot(a.astype(BF), vh) + _dot_nt(q_in[:, ks], st.astype(BF))
            s_scr[h] = st * dec[:, ks] + _dot_tn(vh, k_end[:, ks])
            o = o * lax.rsqrt(jnp.mean(o * o, axis=-1, keepdims=True) + LN_EPS) * ng
            gh = g_ref[0, rows, vs]
            o_ref[0, rows, vs] = (o * (gh * jax.nn.sigmoid(gh))).astype(BF)

    @pl.when(t == pl.num_programs(1) - 1)
    def _():
        sfin_ref[0] = s_scr[...]


def _gla_rec(q, k, v, g, gk, s0t, norm_g, heads, dk, dv, tm):
    b, t, kd = q.shape
    vd = v.shape[2]
    chunk = GLA_CHUNK
    kern = functools.partial(_gla_rec_kernel, heads=heads, dk=dk, dv=dv, chunk=chunk,
                             n_chunks=tm // chunk)
    st_spec = pl.BlockSpec((1, heads, dv, dk), lambda bb, tt: (bb, 0, 0, 0))
    return pl.pallas_call(
        kern,
        out_shape=[jax.ShapeDtypeStruct((b, t, vd), BF),
                   jax.ShapeDtypeStruct((b, heads, dv, dk), F32)],
        grid=(b, t // tm),
        in_specs=[_row_spec(tm, kd), _row_spec(tm, kd), _row_spec(tm, vd), _row_spec(tm, vd),
                  _row_spec(tm, kd), st_spec, _const_spec(norm_g.shape)],
        out_specs=[_row_spec(tm, vd), st_spec],
        scratch_shapes=[pltpu.VMEM((heads, dv, dk), F32)],
        compiler_params=_params("parallel", "arbitrary"),
        name="gla_rec",
    )(q, k, v, g, gk, s0t, norm_g)


def _out_ln_kernel(x_ref, o_ref, gate_ref, w_ref, lng_ref, lnb_ref, out_ref, *, alpha):
    y = _dot(o_ref[0], w_ref[...])
    z = alpha * x_ref[0] + (1.0 + gate_ref[0]) * y
    out_ref[0] = _layer_norm(z, lng_ref[...], lnb_ref[...])


def _out_ln(x, o, gate, w, lng, lnb, alpha, tm):
    b, t, d = x.shape
    kk = o.shape[2]
    return pl.pallas_call(
        functools.partial(_out_ln_kernel, alpha=alpha),
        out_shape=jax.ShapeDtypeStruct((b, t, d), F32),
        grid=(b, t // tm),
        in_specs=[_row_spec(tm, d), _row_spec(tm, kk), _mod_spec(gate, tm), _const_spec(w.shape),
                  _const_spec(lng.shape), _const_spec(lnb.shape)],
        out_specs=_row_spec(tm, d),
        compiler_params=_params("parallel", "parallel"),
        name="out_ln",
    )(x, o, gate, w, lng, lnb)


def _ffn_kernel(*refs, alpha, ff, fc, tm, seq_len):
    if seq_len is None:
        (x_ref, sh_ref, sc_ref, gate_ref, st_ref, wup_ref, wconv_ref, bconv_ref, wdown_ref,
         lng_ref, lnb_ref, out_ref, tail_ref, carry) = refs
    else:
        (x_ref, sh_ref, sc_ref, gate_ref, tap1_ref, tap2_ref, wup_ref, wconv_ref, bconv_ref,
         wdown_ref, lng_ref, lnb_ref, out_ref, tail_ref) = refs
    t = pl.program_id(1)
    x = x_ref[0]
    h = (x * (1.0 + sc_ref[0]) + sh_ref[0]).astype(BF)
    rowi = lax.broadcasted_iota(jnp.int32, (tm, fc), 0)
    acc = jnp.zeros((tm, x.shape[1]), F32)
    for c in range(ff // fc):
        cs = slice(c * fc, (c + 1) * fc)
        u = _dot(h, wup_ref[:, c * fc:(c + 1) * fc])
        g = _dot(h, wup_ref[:, ff + c * fc:ff + (c + 1) * fc])
        g1 = pltpu.roll(g, 1, 0)
        g2 = pltpu.roll(g, 2, 0)
        if seq_len is None:
            prev = jnp.where(t == 0, st_ref[0, :, cs], carry[0:2, cs])
            g1 = jnp.where(rowi == 0, prev[1:2, :], g1)
            g2 = jnp.where(rowi == 0, prev[0:1, :], jnp.where(rowi == 1, prev[1:2, :], g2))
            carry[0:2, cs] = g[tm - 2:tm, :]
            tail_ref[0, :, cs] = g[tm - 2:tm, :]
        else:
            pos = rowi & (seq_len - 1)
            g1 = jnp.where(pos >= 1, g1, tap1_ref[0, :, cs])
            g2 = jnp.where(pos >= 2, g2, tap2_ref[0, :, cs])
            tail_ref[0, :, cs] = g
        gc = (bconv_ref[:, cs] + g2 * wconv_ref[0:1, cs] + g1 * wconv_ref[1:2, cs]
              + g * wconv_ref[2:3, cs])
        a = 0.5 * gc * (1.0 + lax.erf(gc * 0.7071067811865476)) * u
        acc = acc + _dot(a.astype(BF), wdown_ref[c * fc:(c + 1) * fc, :])
    z = alpha * x + (1.0 + gate_ref[0]) * acc
    out_ref[0] = _layer_norm(z, lng_ref[...], lnb_ref[...])


def _ffn(x, sh, sc, gate, taps, w_up, w_conv, b_conv, w_down, lng, lnb, alpha, tm, seq_len):
    b, t, d = x.shape
    ff = w_down.shape[0]
    fc = 256 if ff % 256 == 0 else LANES
    kern = functools.partial(_ffn_kernel, alpha=alpha, ff=ff, fc=fc, tm=tm, seq_len=seq_len)
    wspecs = [_const_spec(w_up.shape), _const_spec(w_conv.shape), _const_spec(b_conv.shape),
              _const_spec(w_down.shape), _const_spec(lng.shape), _const_spec(lnb.shape)]
    if seq_len is None:
        tap_specs = [pl.BlockSpec((1, 2, ff), lambda bb, tt: (bb, 0, 0))]
        tail_shape = jax.ShapeDtypeStruct((b, 2, ff), F32)
        tail_spec = pl.BlockSpec((1, 2, ff), lambda bb, tt: (bb, 0, 0))
        scratch = [pltpu.VMEM((8, ff), F32)]
        sem = ("parallel", "arbitrary")
    else:
        assert seq_len & (seq_len - 1) == 0 and tm % seq_len == 0
        tap_specs = [_row_spec(tm, ff), _row_spec(tm, ff)]
        tail_shape = jax.ShapeDtypeStruct((b, t, ff), F32)
        tail_spec = _row_spec(tm, ff)
        scratch = []
        sem = ("parallel", "parallel")
    return pl.pallas_call(
        kern,
        out_shape=[jax.ShapeDtypeStruct((b, t, d), F32), tail_shape],
        grid=(b, t // tm),
        in_specs=[_row_spec(tm, d), _mod_spec(sh, tm), _mod_spec(sc, tm), _mod_spec(gate, tm)]
        + tap_specs + wspecs,
        out_specs=[_row_spec(tm, d), tail_spec],
        scratch_shapes=scratch,
        compiler_params=_params(*sem),
        name="conv_ffn",
    )(x, sh, sc, gate, *taps, w_up, w_conv, b_conv, w_down, lng, lnb)


def _rope_store(dst_ref, x, cos, sin, hd):
    half = hd // 2
    lane = lax.broadcasted_iota(jnp.int32, (x.shape[0], LANES), 1)
    first = (lane & (hd - 1)) < half
    for s in range(x.shape[1] // LANES):
        xs = x[:, s * LANES:(s + 1) * LANES]
        partner = jnp.where(first, pltpu.roll(xs, LANES - half, 1), pltpu.roll(xs, half, 1))
        dst_ref[0, :, s * LANES:(s + 1) * LANES] = xs * cos + partner * sin


def _kvq_kernel(*refs, dm, hd, transposed):
    if transposed:
        (x_ref, shk_ref, sck_ref, shq_ref, scq_ref, wk_ref, wv_ref, wq_ref, cos_ref, sin_ref,
         cos_t_ref, sin_t_ref, k_ref, v_ref, q_ref) = refs
    else:
        (x_ref, shk_ref, sck_ref, shq_ref, scq_ref, wk_ref, wv_ref, wq_ref, cos_ref, sin_ref,
         k_ref, v_ref, q_ref) = refs
    x = x_ref[0]
    hk = (x * (1.0 + sck_ref[0]) + shk_ref[0]).astype(BF)
    hq = (x * (1.0 + scq_ref[0]) + shq_ref[0]).astype(BF)
    cos = cos_ref[...]
    sin = sin_ref[...]
    if transposed:
        tm = x.shape[0]
        heads, half = dm // hd, hd // 2
        k3 = _dot_nt(wk_ref[...], hk).reshape(heads, hd, tm)
        partner = jnp.concatenate([k3[:, half:, :], k3[:, :half, :]], axis=1)
        k_ref[0] = k3 * cos_t_ref[...][None] + partner * sin_t_ref[...][None]
        v_ref[0] = _dot_nt(wv_ref[...], hk).reshape(heads, hd, tm)
    else:
        _rope_store(k_ref, _dot(hk, wk_ref[...]), cos, sin, hd)
        v_ref[0] = _dot(hk, wv_ref[...])
    _rope_store(q_ref, _dot(hq, wq_ref[...]), cos, sin, hd)


def _kvq(x, shk, sck, shq, scq, wk, wv, wq, tables, hd, tm, transposed):
    b, t, d = x.shape
    dm = wq.shape[1]
    heads = dm // hd
    kern = functools.partial(_kvq_kernel, dm=dm, hd=hd, transposed=transposed)
    cos, sin, cos_t, sin_t = tables
    tab_spec = pl.BlockSpec((tm, LANES), lambda bb, tt: (tt, 0))
    in_specs = [_row_spec(tm, d), _mod_spec(shk, tm), _mod_spec(sck, tm), _mod_spec(shq, tm),
                _mod_spec(scq, tm), _const_spec(wk.shape), _const_spec(wv.shape),
                _const_spec(wq.shape), tab_spec, tab_spec]
    args = [x, shk, sck, shq, scq, wk, wv, wq, cos, sin]
    if transposed:
        tab_t_spec = pl.BlockSpec((hd, tm), lambda bb, tt: (0, tt))
        in_specs += [tab_t_spec, tab_t_spec]
        args += [cos_t, sin_t]
        kv_shape = jax.ShapeDtypeStruct((b, heads, hd, t), F32)
        kv_spec = pl.BlockSpec((1, heads, hd, tm), lambda bb, tt: (bb, 0, 0, tt))
    else:
        kv_shape = jax.ShapeDtypeStruct((b, t, dm), F32)
        kv_spec = _row_spec(tm, dm)
    return pl.pallas_call(
        kern,
        out_shape=[kv_shape, kv_shape, jax.ShapeDtypeStruct((b, t, dm), F32)],
        grid=(b, t // tm),
        in_specs=in_specs,
        out_specs=[kv_spec, kv_spec, _row_spec(tm, dm)],
        compiler_params=_params("parallel", "parallel"),
        name="kvq_proj",
    )(*args)


def _rope_tables(pos, hd):
    half = hd // 2
    inv = ROPE_THETA ** (-jnp.arange(half, dtype=F32) / half)
    ang = pos.astype(F32)[:, None] * inv[None, :]
    cos, sin = jnp.cos(ang), jnp.sin(ang)
    cos_h = jnp.concatenate([cos, cos], axis=1)
    sin_h = jnp.concatenate([-sin, sin], axis=1)
    reps = LANES // hd
    return jnp.tile(cos_h, (1, reps)), jnp.tile(sin_h, (1, reps)), cos_h.T, sin_h.T


def _select_bias(gate, n_valid, topk):
    bidx = lax.broadcasted_iota(jnp.int32, gate.shape, 1)
    bidx_f = bidx.astype(F32)
    ninf = jnp.float32(-jnp.inf)
    gate = jnp.where(bidx < n_valid, gate, ninf)
    sel = jnp.zeros(gate.shape, jnp.bool_)
    for _ in range(topk):
        mx = jnp.max(gate, axis=-1, keepdims=True)
        first = jnp.min(jnp.where(gate == mx, bidx_f, jnp.float32(gate.shape[1])), axis=-1,
                        keepdims=True)
        pick = (bidx_f == first) & (mx > ninf)
        sel = sel | pick
        gate = jnp.where(pick, ninf, gate)
    return jnp.where(sel, 0.0, MASKED)


def _gate_scores(q, km):
    return jnp.dot(q, km, preferred_element_type=F32, precision=lax.Precision.HIGHEST)


def _moba_prompt_kernel(q_ref, k_ref, v_ref, o_ref, qb_scr, bias_scr, km_scr, m_scr, l_scr,
                        acc_scr, *, hd, blk, topk, scale, n_blocks):
    i = pl.program_id(2)
    n_heads = LANES // hd

    def kv_tile(ref, start):
        return ref[0, :, :, pl.ds(start, blk)].reshape(LANES, blk).astype(BF)

    @pl.when(i == 0)
    def _():
        km_scr[...] = jnp.zeros_like(km_scr)
        for j in range(n_blocks):
            kb = k_ref[0, :, :, j * blk:(j + 1) * blk].reshape(LANES, blk)
            km_scr[:, j:j + 1] = jnp.mean(kb, axis=-1, keepdims=True)

    q = q_ref[0]
    lane = lax.broadcasted_iota(jnp.int32, q.shape, 1)
    row = lax.broadcasted_iota(jnp.int32, (blk, blk), 0)
    col = lax.broadcasted_iota(jnp.int32, (blk, blk), 1)
    own = pl.multiple_of(i * blk, blk)
    k_own = kv_tile(k_ref, own)
    v_own = kv_tile(v_ref, own)
    for hh in range(n_heads):
        qh = jnp.where((lane >= hh * hd) & (lane < (hh + 1) * hd), q, 0.0)
        bias_scr[hh] = _select_bias(_gate_scores(qh, km_scr[...]), i, topk).astype(BF)
        qb = (qh * scale).astype(BF)
        qb_scr[hh] = qb
        s = jnp.where(col <= row, _dot(qb, k_own), MASKED)
        m = jnp.max(s, axis=-1, keepdims=True)
        p = jnp.exp(s - m)
        m_scr[hh] = m
        l_scr[hh] = jnp.sum(p, axis=-1, keepdims=True)
        acc_scr[hh] = _dot_nt(p.astype(BF), v_own)

    def body(j, carry):
        start = pl.multiple_of(j * blk, blk)
        kj = kv_tile(k_ref, start)
        vj = kv_tile(v_ref, start)
        onehot = (lax.broadcasted_iota(jnp.int32, (LANES, blk), 0) == j).astype(BF)
        for hh in range(n_heads):
            s = _dot(qb_scr[hh], kj) + _dot(bias_scr[hh], onehot)
            m_prev = m_scr[hh]
            m_new = jnp.maximum(m_prev, jnp.max(s, axis=-1, keepdims=True))
            alpha = jnp.exp(m_prev - m_new)
            p = jnp.exp(s - m_new)
            l_scr[hh] = alpha * l_scr[hh] + jnp.sum(p, axis=-1, keepdims=True)
            acc_scr[hh] = alpha * acc_scr[hh] + _dot_nt(p.astype(BF), vj)
            m_scr[hh] = m_new
        return carry

    lax.fori_loop(0, i, body, 0)
    out = jnp.zeros(q.shape, F32)
    for hh in range(n_heads):
        out = jnp.where((lane >= hh * hd) & (lane < (hh + 1) * hd), acc_scr[hh] / l_scr[hh], out)
    o_ref[0] = out.astype(o_ref.dtype)


def _moba_prompt(q, k_t, v_t, hd):
    b, t, dm = q.shape
    blk = MOBA_BLOCK
    n_blocks = t // blk
    n_heads = LANES // hd
    assert n_blocks <= LANES
    kern = functools.partial(_moba_prompt_kernel, hd=hd, blk=blk, topk=MOBA_TOPK,
                             scale=float(hd) ** -0.5, n_blocks=n_blocks)
    seq_spec = pl.BlockSpec((1, n_heads, hd, t), lambda bb, hp, i: (bb, hp, 0, 0))
    tile_spec = pl.BlockSpec((1, blk, LANES), lambda bb, hp, i: (bb, i, hp))
    return pl.pallas_call(
        kern,
        out_shape=jax.ShapeDtypeStruct((b, t, dm), BF),
        grid=(b, dm // LANES, n_blocks),
        in_specs=[tile_spec, seq_spec, seq_spec],
        out_specs=tile_spec,
        scratch_shapes=[pltpu.VMEM((n_heads, blk, LANES), BF),
                        pltpu.VMEM((n_heads, blk, LANES), BF),
                        pltpu.VMEM((LANES, LANES), F32),
                        pltpu.VMEM((n_heads, blk, 1), F32),
                        pltpu.VMEM((n_heads, blk, 1), F32),
                        pltpu.VMEM((n_heads, blk, LANES), F32)],
        compiler_params=_params("parallel", "parallel", "arbitrary"),
        name="moba_prompt",
    )(q, k_t, v_t)


def _page_specs(heads, hd, page):
    even = pl.BlockSpec((1, heads, hd, page), lambda ss, j, pt: (pt[ss, 2 * j], 0, 0, 0))
    odd = pl.BlockSpec((1, heads, hd, page), lambda ss, j, pt: (pt[ss, 2 * j + 1], 0, 0, 0))
    return even, odd


def _page_mean_kernel(pt_ref, ke_ref, ko_ref, o_ref, *, blk):
    j = pl.program_id(1)

    @pl.when(j == 0)
    def _():
        o_ref[...] = jnp.zeros_like(o_ref)

    dm, n_cols = o_ref.shape[1], o_ref.shape[2]
    page = ke_ref.shape[3]
    tot = jnp.sum(ke_ref[0].reshape(dm, page) + ko_ref[0].reshape(dm, page), axis=-1,
                  keepdims=True) * (1.0 / blk)
    lane = lax.broadcasted_iota(jnp.int32, (dm, n_cols), 1)
    o_ref[0] = jnp.where(lane == j, tot, o_ref[0])


def _page_mean(page_table, cache_k_t):
    s, n_pages = page_table.shape
    _, heads, hd, page = cache_k_t.shape
    assert 2 * page == MOBA_BLOCK and n_pages // 2 <= LANES
    dm = heads * hd
    even, odd = _page_specs(heads, hd, page)
    grid_spec = pltpu.PrefetchScalarGridSpec(
        num_scalar_prefetch=1,
        grid=(s, n_pages // 2),
        in_specs=[even, odd],
        out_specs=pl.BlockSpec((1, dm, LANES), lambda ss, j, pt: (ss, 0, 0)),
    )
    return pl.pallas_call(
        functools.partial(_page_mean_kernel, blk=MOBA_BLOCK),
        out_shape=jax.ShapeDtypeStruct((s, dm, LANES), F32),
        grid_spec=grid_spec,
        compiler_params=_params("parallel", "arbitrary"),
        name="page_mean",
    )(page_table, cache_k_t, cache_k_t)


def _moba_decode_kernel(pt_ref, q_ref, km_ref, kn_ref, vn_ref, ke_ref, ko_ref, ve_ref, vo_ref,
                        o_ref, qb_scr, bias_scr, m_scr, l_scr, acc_scr,
                        *, hd, n_heads, qpad, n_new, n_blocks, topk, scale):
    j = pl.program_id(1)
    dm = q_ref.shape[2]
    page = ke_ref.shape[3]
    lane = lax.broadcasted_iota(jnp.int32, (qpad, dm), 1)

    @pl.when(j == 0)
    def _():
        q = q_ref[0]
        qm = jnp.concatenate(
            [jnp.where((lane >= hh * hd) & (lane < (hh + 1) * hd), q, 0.0)
             for hh in range(n_heads)], axis=0)
        bias_scr[...] = _select_bias(_gate_scores(qm, km_ref[0]), n_blocks, topk).astype(BF)
        qb = (qm * scale).astype(BF)
        qb_scr[...] = qb
        s = _dot_nt(qb, kn_ref[0].astype(BF))
        qpos = lax.broadcasted_iota(jnp.int32, s.shape, 0) & (qpad - 1)
        kpos = lax.broadcasted_iota(jnp.int32, s.shape, 1)
        s = jnp.where((kpos <= qpos) & (kpos < n_new), s, MASKED)
        m = jnp.max(s, axis=-1, keepdims=True)
        p = jnp.exp(s - m)
        m_scr[...] = m
        l_scr[...] = jnp.sum(p, axis=-1, keepdims=True)
        acc_scr[...] = _dot(p.astype(BF), vn_ref[0].astype(BF))

    qb = qb_scr[...]
    ke = ke_ref[0].reshape(dm, page).astype(BF)
    ko = ko_ref[0].reshape(dm, page).astype(BF)
    onehot = (lax.broadcasted_iota(jnp.int32, (bias_scr.shape[1], 2 * page), 0) == j).astype(BF)
    s = jnp.concatenate([_dot(qb, ke), _dot(qb, ko)], axis=-1) + _dot(bias_scr[...], onehot)
    m_prev = m_scr[...]
    m_new = jnp.maximum(m_prev, jnp.max(s, axis=-1, keepdims=True))
    alpha = jnp.exp(m_prev - m_new)
    p = jnp.exp(s - m_new)
    l_scr[...] = alpha * l_scr[...] + jnp.sum(p, axis=-1, keepdims=True)
    pb = p.astype(BF)
    pv = (_dot_nt(pb[:, :page], ve_ref[0].reshape(dm, page).astype(BF))
          + _dot_nt(pb[:, page:], vo_ref[0].reshape(dm, page).astype(BF)))
    acc_scr[...] = alpha * acc_scr[...] + pv
    m_scr[...] = m_new

    @pl.when(j == pl.num_programs(1) - 1)
    def _():
        res = acc_scr[...] / l_scr[...]
        out = jnp.zeros((qpad, dm), F32)
        for hh in range(n_heads):
            out = jnp.where((lane >= hh * hd) & (lane < (hh + 1) * hd),
                            res[hh * qpad:(hh + 1) * qpad, :], out)
        o_ref[0] = out.astype(o_ref.dtype)


def _moba_decode(page_table, q, kmean, k_new, v_new, cache_k_t, cache_v_t, n_new):
    s, qpad, dm = q.shape
    _, n_heads, hd, page = cache_k_t.shape
    n_blocks = page_table.shape[1] // 2
    rows = n_heads * qpad
    kern = functools.partial(_moba_decode_kernel, hd=hd, n_heads=n_heads, qpad=qpad, n_new=n_new,
                             n_blocks=n_blocks, topk=MOBA_TOPK, scale=float(hd) ** -0.5)
    even, odd = _page_specs(n_heads, hd, page)
    per_seq = lambda a: pl.BlockSpec((1,) + a.shape[1:], lambda ss, j, pt: (ss, 0, 0))
    grid_spec = pltpu.PrefetchScalarGridSpec(
        num_scalar_prefetch=1,
        grid=(s, n_blocks),
        in_specs=[per_seq(q), per_seq(kmean), per_seq(k_new), per_seq(v_new),
                  even, odd, even, odd],
        out_specs=per_seq(q),
        scratch_shapes=[pltpu.VMEM((rows, dm), BF),
                        pltpu.VMEM((rows, LANES), BF),
                        pltpu.VMEM((rows, 1), F32),
                        pltpu.VMEM((rows, 1), F32),
                        pltpu.VMEM((rows, dm), F32)],
    )
    return pl.pallas_call(
        kern,
        out_shape=jax.ShapeDtypeStruct((s, qpad, dm), BF),
        grid_spec=grid_spec,
        compiler_params=_params("parallel", "arbitrary"),
        name="moba_decode",
    )(page_table, q, kmean, k_new, v_new, cache_k_t, cache_k_t, cache_v_t, cache_v_t)


def _pad_rows(a, n, axis=1):
    pad = [(0, 0)] * a.ndim
    pad[axis] = (0, n - a.shape[axis])
    return jnp.pad(a, pad)


def _trunk(x, mods, kv_mod, gla_fns, conv_taps, seq_len, kvq_fn, attend, w, tiles):
    depth = len(mods)
    n_a = depth // 2
    alpha = float((2 * depth) ** 0.25)
    heads, dk, dv = w["gla_dims"]
    gla_states, conv_tails = [], []
    k_sh = v_sh = ctx = None
    for i in range(depth):
        sh_m, sc_m, g_m, sh_f, sc_f, g_f = mods[i]
        if i < n_a:
            q, k, v, g, gk = _gla_proj(x, sh_m, sc_m, w["a_w_in"][i], w["a_w_gk1"][i],
                                       w["a_w_gk2"][i], w["a_b_gk"][i], heads * dk, heads * dv,
                                       float(dk) ** -0.5, tiles["proj"])
            o, s_fin = gla_fns[i](q, k, v, g, gk)
            gla_states.append(s_fin)
            w_out = w["a_w_out"][i]
        else:
            o = attend(ctx)
            w_out = w["b_w_out"][i - n_a]
        x = _out_ln(x, o, g_m, w_out, w["ln_g"][i, 0:1], w["ln_b"][i, 0:1], alpha, tiles["row"])
        x, tail = _ffn(x, sh_f, sc_f, g_f, conv_taps[i], w["w_up"][i], w["w_conv"][i],
                       w["b_conv"][i:i + 1], w["w_down"][i], w["ln_g"][i, 1:2], w["ln_b"][i, 1:2],
                       alpha, tiles["ffn"], seq_len)
        conv_tails.append(tail)
        if i == n_a - 1:
            k_sh, v_sh, ctx = kvq_fn(x, kv_mod[0], kv_mod[1], mods[n_a][0], mods[n_a][1])
    return x, k_sh, v_sh, gla_states, conv_tails


def kernel(x_prompt, x_sample, cache_k, cache_v, state_gla, state_ffn_conv, page_table, c_prompt,
           c_sample, w_ada, b_ada, ln_g, ln_b, w_up, w_conv, b_conv, w_down, a_w_in, a_w_gk1,
           a_w_gk2, a_b_gk, a_norm_g, a_w_out, kv_w_ada, kv_b_ada, kv_w, b_w_q, b_w_out):
    bp, tp, d = x_prompt.shape
    bs, ts, _ = x_sample.shape
    depth = w_ada.shape[0]
    n_a = state_gla.shape[0]
    heads, dk, dv = state_gla.shape[2:]
    n_phys, page, m_heads, hd = cache_k.shape
    dm = m_heads * hd
    n_pages = page_table.shape[1]
    past_len = n_pages * page
    ff = w_down.shape[1]
    blk = MOBA_BLOCK
    assert depth == 2 * n_a == 2 and b_w_q.shape[0] == 1
    assert tp % blk == 0 and past_len % blk == 0
    assert 2 <= ts <= 8 and ts & (ts - 1) == 0
    assert LANES % hd == 0 and dm % LANES == 0

    w_k, w_v = kv_w[:, :dm].astype(BF), kv_w[:, dm:].astype(BF)
    w_q = b_w_q[0].astype(BF)
    w = {
        "gla_dims": (heads, dk, dv),
        "a_w_in": a_w_in.astype(BF),
        "a_w_gk1": _pad_rows(a_w_gk1, GLA_GATE_RANK_PAD, axis=2).astype(BF),
        "a_w_gk2": _pad_rows(a_w_gk2, GLA_GATE_RANK_PAD, axis=1).astype(BF),
        "a_b_gk": a_b_gk[:, None, :], "a_w_out": a_w_out.astype(BF),
        "w_up": w_up.astype(BF), "w_conv": w_conv, "b_conv": b_conv, "w_down": w_down.astype(BF),
        "ln_g": ln_g, "ln_b": ln_b, "b_w_out": b_w_out.astype(BF),
    }
    norm_g = [a_norm_g[i][None, :] for i in range(n_a)]

    n_c = bp + bs
    c_all = _pad_rows(jnp.concatenate([c_prompt, c_sample], axis=0), -(-n_c // 8) * 8, axis=0)
    mod_all = [_ada(c_all, w_ada, i, b_ada[i:i + 1]) for i in range(depth)]
    kv_mod_all = _ada(c_all, kv_w_ada[None], 0, kv_b_ada[None, :])

    def per_batch(m, n):
        return [m[:bp, j * d:(j + 1) * d][:, None, :] for j in range(n)]

    def per_token(m, n):
        return [jnp.repeat(m[bp:bp + bs, j * d:(j + 1) * d], ts, axis=0)[None] for j in range(n)]

    def gla_prompt(i):
        s0 = jnp.zeros((bp, heads, dv, dk), F32)
        return lambda q, k, v, g, gk: _gla_rec(q, k, v, g, gk, s0, norm_g[i], heads, dk, dv, 256)

    tables_p = _rope_tables(jnp.arange(tp, dtype=jnp.int32), hd)
    tiles_p = {"proj": 512, "row": 512, "ffn": 512}

    def kvq_prompt(x, shk, sck, shq, scq):
        k_t, v_t, q = _kvq(x, shk, sck, shq, scq, w_k.T, w_v.T, w_q, tables_p, hd, tiles_p["row"],
                           True)
        return k_t, v_t, (q, k_t, v_t)

    zero_taps = [[jnp.zeros((bp, 2, ff), F32)] for _ in range(depth)]
    y_p, k_p, v_p, gla_p, conv_p = _trunk(
        x_prompt, [per_batch(m, 6) for m in mod_all], per_batch(kv_mod_all, 2),
        [gla_prompt(i) for i in range(n_a)], zero_taps, None, kvq_prompt,
        lambda ctx: _moba_prompt(*ctx, hd), w, tiles_p)

    n_rows = bs * ts
    xs = x_sample.reshape(1, n_rows, d)
    pos_s = past_len + jnp.arange(ts, dtype=jnp.int32)

    def gla_sample(i):
        s0 = jnp.swapaxes(state_gla[i], -1, -2)

        def run(q, k, v, g, gk):
            def seqs(a):
                return _pad_rows(a.reshape(bs, ts, a.shape[-1]), GLA_CHUNK)
            o, s_fin = _gla_rec(seqs(q), seqs(k), seqs(v), seqs(g), seqs(gk), s0, norm_g[i],
                                heads, dk, dv, GLA_CHUNK)
            return o[:, :ts].reshape(1, n_rows, heads * dv), s_fin
        return run

    qpad = 8
    cache_k_t = jnp.transpose(cache_k, (0, 2, 3, 1))
    cache_v_t = jnp.transpose(cache_v, (0, 2, 3, 1))
    tables_s = _rope_tables(jnp.tile(pos_s, bs), hd)
    tiles_s = {"proj": n_rows, "row": n_rows, "ffn": n_rows}

    def kvq_sample(x, shk, sck, shq, scq):
        k, v, q = _kvq(x, shk, sck, shq, scq, w_k, w_v, w_q, tables_s, hd, tiles_s["row"], False)
        return k, v, (q, k, v)

    def attend_sample(ctx):
        q, k, v = ctx
        kmean = _page_mean(page_table, cache_k_t)
        seqs = lambda a, n: _pad_rows(a.reshape(bs, ts, dm), n)
        o = _moba_decode(page_table, seqs(q, qpad), kmean, seqs(k, LANES), seqs(v, LANES),
                         cache_k_t, cache_v_t, ts)
        return o[:, :ts].reshape(1, n_rows, dm)

    def taps(i):
        st = state_ffn_conv[i]
        zeros = jnp.zeros((bs, ts - 1, ff), F32)
        tap1 = jnp.concatenate([st[:, 1:2], zeros], axis=1)
        tap2 = jnp.concatenate([st[:, 0:1], st[:, 1:2], zeros[:, :ts - 2]], axis=1)
        return [tap1.reshape(1, n_rows, ff), tap2.reshape(1, n_rows, ff)]

    y_s, k_s, v_s, gla_s, conv_s = _trunk(
        xs, [per_token(m, 6) for m in mod_all], per_token(kv_mod_all, 2),
        [gla_sample(i) for i in range(n_a)], [taps(i) for i in range(depth)], ts, kvq_sample,
        attend_sample, w, tiles_s)

    gla_state_p = jnp.stack([jnp.swapaxes(s, -1, -2) for s in gla_p])
    gla_state_s = jnp.stack([jnp.swapaxes(s, -1, -2) for s in gla_s])
    conv_state_p = jnp.stack(conv_p)
    conv_state_s = jnp.stack([g.reshape(bs, ts, ff)[:, ts - 2:] for g in conv_s])
    return (y_p, y_s.reshape(bs, ts, d),
            jnp.transpose(k_p, (0, 3, 1, 2)), jnp.transpose(v_p, (0, 3, 1, 2)),
            k_s.reshape(bs, ts, m_heads, hd), v_s.reshape(bs, ts, m_heads, hd),
            gla_state_p, gla_state_s, conv_state_p, conv_state_s)
```

```python
import functools

import jax
import jax.numpy as jnp
from jax import lax
from jax.experimental import pallas as pl
from jax.experimental.pallas import tpu as pltpu

F32 = jnp.float32
BF = jnp.bfloat16

LN_EPS = 1e-5
GLA_CHUNK = 64
GLA_GATE_NORM = 16.0
GLA_GATE_RANK_PAD = 128
MOBA_BLOCK = 256
MOBA_TOPK = 3
ROPE_THETA = 10000.0
MASKED = -1e30
LOG2_E = 1.4426950408889634
SUM_ROWS = 16
LANES = 128
VMEM_LIMIT = 56 * 1024 * 1024


def _params(*sem):
    return pltpu.CompilerParams(dimension_semantics=sem, vmem_limit_bytes=VMEM_LIMIT)


def _dot(a, b):
    return jnp.dot(a, b, preferred_element_type=F32)


def _dot_nt(a, b):
    return lax.dot_general(a, b, (((1,), (1,)), ((), ())), preferred_element_type=F32)


def _dot_tn(a, b):
    return lax.dot_general(a, b, (((0,), (0,)), ((), ())), preferred_element_type=F32)


def _layer_norm(z, g, b):
    mu = jnp.mean(z, axis=-1, keepdims=True)
    zc = z - mu
    var = jnp.mean(zc * zc, axis=-1, keepdims=True)
    return zc * lax.rsqrt(var + LN_EPS) * g + b


def _const_spec(shape):
    nd = len(shape)
    return pl.BlockSpec(shape, lambda *_: (0,) * nd)


def _mod_spec(arr, tm):
    if arr.shape[1] == 1:
        return pl.BlockSpec((1, 1, arr.shape[2]), lambda b, t: (b, 0, 0))
    return pl.BlockSpec((1, tm, arr.shape[2]), lambda b, t: (b, t, 0))


def _row_spec(tm, width):
    return pl.BlockSpec((1, tm, width), lambda b, t: (b, t, 0))


def _ada_kernel(c_ref, w_ref, b_ref, o_ref):
    c = c_ref[...]
    s = c * jax.nn.sigmoid(c)
    o_ref[...] = _dot(s.astype(BF), w_ref[...].astype(BF)) + b_ref[...]


def _ada(c, w3, layer, b2, tn=1024):
    r, d = c.shape
    n = w3.shape[2]
    return pl.pallas_call(
        _ada_kernel,
        out_shape=jax.ShapeDtypeStruct((r, n), F32),
        grid=(n // tn,),
        in_specs=[pl.BlockSpec((r, d), lambda j: (0, 0)),
                  pl.BlockSpec((None, d, tn), lambda j: (layer, 0, j)),
                  pl.BlockSpec((1, tn), lambda j: (0, j))],
        out_specs=pl.BlockSpec((r, tn), lambda j: (0, j)),
        compiler_params=_params("arbitrary"),
        name="ada_mod",
    )(c, w3, b2)


def _gla_proj_kernel(x_ref, sh_ref, sc_ref, win_ref, wg1_ref, wg2_ref, bgk_ref,
                     q_ref, k_ref, v_ref, g_ref, gk_ref, *, kd, vd, scale):
    h = (x_ref[0] * (1.0 + sc_ref[0]) + sh_ref[0]).astype(BF)
    q_ref[0] = _dot(h, win_ref[:, 0:kd]) * scale
    k_ref[0] = _dot(h, win_ref[:, kd:2 * kd])
    v_ref[0] = _dot(h, win_ref[:, 2 * kd:2 * kd + vd]).astype(BF)
    g_ref[0] = _dot(h, win_ref[:, 2 * kd + vd:2 * kd + 2 * vd])
    r = _dot(h, wg1_ref[...])
    lin = _dot(r.astype(BF), wg2_ref[...]) + bgk_ref[...]
    gk_ref[0] = (jnp.minimum(lin, 0.0) - jnp.log1p(jnp.exp(-jnp.abs(lin)))) * (1.0 / GLA_GATE_NORM)


def _gla_proj(x, sh, sc, w_in, wg1, wg2, bgk, kd, vd, scale, tm):
    b, t, d = x.shape
    kern = functools.partial(_gla_proj_kernel, kd=kd, vd=vd, scale=scale)
    return pl.pallas_call(
        kern,
        out_shape=[jax.ShapeDtypeStruct((b, t, kd), F32), jax.ShapeDtypeStruct((b, t, kd), F32),
                   jax.ShapeDtypeStruct((b, t, vd), BF), jax.ShapeDtypeStruct((b, t, vd), F32),
                   jax.ShapeDtypeStruct((b, t, kd), F32)],
        grid=(b, t // tm),
        in_specs=[_row_spec(tm, d), _mod_spec(sh, tm), _mod_spec(sc, tm),
                  _const_spec(w_in.shape), _const_spec(wg1.shape), _const_spec(wg2.shape),
                  _const_spec(bgk.shape)],
        out_specs=[_row_spec(tm, kd), _row_spec(tm, kd), _row_spec(tm, vd), _row_spec(tm, vd),
                   _row_spec(tm, kd)],
        compiler_params=_params("parallel", "parallel"),
        name="gla_proj",
    )(x, sh, sc, w_in, wg1, wg2, bgk)


def _split3(x):
    hi = x.astype(BF)
    r = x - hi.astype(F32)
    mid = r.astype(BF)
    lo = (r - mid.astype(F32)).astype(BF)
    return hi, mid, lo


def _gla_rec_kernel(q_ref, k_ref, v_ref, g_ref, gk_ref, s0_ref, ng_ref, o_ref, sfin_ref, s_scr,
                    *, heads, dk, dv, chunk, n_chunks):
    t = pl.program_id(1)

    @pl.when(t == 0)
    def _():
        s_scr[...] = s0_ref[0]

    row = lax.broadcasted_iota(jnp.int32, (chunk, chunk), 0)
    col = lax.broadcasted_iota(jnp.int32, (chunk, chunk), 1)
    tril_f = (row >= col).astype(F32)
    tril = tril_f.astype(BF)
    ng = ng_ref[...]

    for c in range(n_chunks):
        rows = pl.ds(c * chunk, chunk)
        hi, mid, lo = _split3(gk_ref[0, rows, :])
        cum = _dot(tril, hi) + _dot(tril, mid) + _dot(tril, lo)
        last = cum[chunk - 1:chunk, :]
        q_in = (q_ref[0, rows, :] * jnp.exp(cum)).astype(BF)
        kc = k_ref[0, rows, :]
        k_in = (kc * jnp.exp(-cum)).astype(BF)
        k_end = (kc * jnp.exp(last - cum)).astype(BF)
        dec = jnp.exp(last)
        for h in range(heads):
            ks = slice(h * dk, (h + 1) * dk)
            vs = slice(h * dv, (h + 1) * dv)
            a = _dot_nt(q_in[:, ks], k_in[:, ks]) * tril_f
            vh = v_ref[0, rows, vs]
            st = s_scr[h]
            o = _dot(a.astype(BF), vh) + _dot_nt(q_in[:, ks], st.astype(BF))
            s_scr[h] = st * dec[:, ks] + _dot_tn(vh, k_end[:, ks])
            o = o * lax.rsqrt(jnp.mean(o * o, axis=-1, keepdims=True) + LN_EPS) * ng
            gh = g_ref[0, rows, vs]
            o_ref[0, rows, vs] = (o * (gh * jax.nn.sigmoid(gh))).astype(BF)

    @pl.when(t == pl.num_programs(1) - 1)
    def _():
        sfin_ref[0] = s_scr[...]


def _gla_rec(q, k, v, g, gk, s0t, norm_g, heads, dk, dv, tm):
    b, t, kd = q.shape
    vd = v.shape[2]
    chunk = GLA_CHUNK
    kern = functools.partial(_gla_rec_kernel, heads=heads, dk=dk, dv=dv, chunk=chunk,
                             n_chunks=tm // chunk)
    st_spec = pl.BlockSpec((1, heads, dv, dk), lambda bb, tt: (bb, 0, 0, 0))
    return pl.pallas_call(
        kern,
        out_shape=[jax.ShapeDtypeStruct((b, t, vd), BF),
                   jax.ShapeDtypeStruct((b, heads, dv, dk), F32)],
        grid=(b, t // tm),
        in_specs=[_row_spec(tm, kd), _row_spec(tm, kd), _row_spec(tm, vd), _row_spec(tm, vd),
                  _row_spec(tm, kd), st_spec, _const_spec(norm_g.shape)],
        out_specs=[_row_spec(tm, vd), st_spec],
        scratch_shapes=[pltpu.VMEM((heads, dv, dk), F32)],
        compiler_params=_params("parallel", "arbitrary"),
        name="gla_rec",
    )(q, k, v, g, gk, s0t, norm_g)


def _out_ln_kernel(x_ref, o_ref, gate_ref, w_ref, lng_ref, lnb_ref, out_ref, *, alpha):
    y = _dot(o_ref[0], w_ref[...])
    z = alpha * x_ref[0] + (1.0 + gate_ref[0]) * y
    out_ref[0] = _layer_norm(z, lng_ref[...], lnb_ref[...])


def _out_ln(x, o, gate, w, lng, lnb, alpha, tm):
    b, t, d = x.shape
    kk = o.shape[2]
    return pl.pallas_call(
        functools.partial(_out_ln_kernel, alpha=alpha),
        out_shape=jax.ShapeDtypeStruct((b, t, d), F32),
        grid=(b, t // tm),
        in_specs=[_row_spec(tm, d), _row_spec(tm, kk), _mod_spec(gate, tm), _const_spec(w.shape),
                  _const_spec(lng.shape), _const_spec(lnb.shape)],
        out_specs=_row_spec(tm, d),
        compiler_params=_params("parallel", "parallel"),
        name="out_ln",
    )(x, o, gate, w, lng, lnb)


def _ffn_kernel(*refs, alpha, ff, fc, tm, seq_len):
    if seq_len is None:
        (x_ref, sh_ref, sc_ref, gate_ref, st_ref, wup_ref, wconv_ref, bconv_ref, wdown_ref,
         lng_ref, lnb_ref, out_ref, tail_ref, carry) = refs
    else:
        (x_ref, sh_ref, sc_ref, gate_ref, tap1_ref, tap2_ref, wup_ref, wconv_ref, bconv_ref,
         wdown_ref, lng_ref, lnb_ref, out_ref, tail_ref) = refs
    t = pl.program_id(1)
    x = x_ref[0]
    h = (x * (1.0 + sc_ref[0]) + sh_ref[0]).astype(BF)
    rowi = lax.broadcasted_iota(jnp.int32, (tm, fc), 0)
    acc = jnp.zeros((tm, x.shape[1]), F32)
    for c in range(ff // fc):
        cs = slice(c * fc, (c + 1) * fc)
        u = _dot(h, wup_ref[:, c * fc:(c + 1) * fc])
        g = _dot(h, wup_ref[:, ff + c * fc:ff + (c + 1) * fc])
        g1 = pltpu.roll(g, 1, 0)
        g2 = pltpu.roll(g, 2, 0)
        if seq_len is None:
            prev = jnp.where(t == 0, st_ref[0, :, cs], carry[0:2, cs])
            g1 = jnp.where(rowi == 0, prev[1:2, :], g1)
            g2 = jnp.where(rowi == 0, prev[0:1, :], jnp.where(rowi == 1, prev[1:2, :], g2))
            carry[0:2, cs] = g[tm - 2:tm, :]
            tail_ref[0, :, cs] = g[tm - 2:tm, :]
        else:
            pos = rowi & (seq_len - 1)
            g1 = jnp.where(pos >= 1, g1, tap1_ref[0, :, cs])
            g2 = jnp.where(pos >= 2, g2, tap2_ref[0, :, cs])
            tail_ref[0, :, cs] = g
        gc = (bconv_ref[:, cs] + g2 * wconv_ref[0:1, cs] + g1 * wconv_ref[1:2, cs]
              + g * wconv_ref[2:3, cs])
        a = 0.5 * gc * (1.0 + lax.erf(gc * 0.7071067811865476)) * u
        acc = acc + _dot(a.astype(BF), wdown_ref[c * fc:(c + 1) * fc, :])
    z = alpha * x + (1.0 + gate_ref[0]) * acc
    out_ref[0] = _layer_norm(z, lng_ref[...], lnb_ref[...])


def _ffn(x, sh, sc, gate, taps, w_up, w_conv, b_conv, w_down, lng, lnb, alpha, tm, seq_len):
    b, t, d = x.shape
    ff = w_down.shape[0]
    fc = 256 if ff % 256 == 0 else LANES
    kern = functools.partial(_ffn_kernel, alpha=alpha, ff=ff, fc=fc, tm=tm, seq_len=seq_len)
    wspecs = [_const_spec(w_up.shape), _const_spec(w_conv.shape), _const_spec(b_conv.shape),
              _const_spec(w_down.shape), _const_spec(lng.shape), _const_spec(lnb.shape)]
    if seq_len is None:
        tap_specs = [pl.BlockSpec((1, 2, ff), lambda bb, tt: (bb, 0, 0))]
        tail_shape = jax.ShapeDtypeStruct((b, 2, ff), F32)
        tail_spec = pl.BlockSpec((1, 2, ff), lambda bb, tt: (bb, 0, 0))
        scratch = [pltpu.VMEM((8, ff), F32)]
        sem = ("parallel", "arbitrary")
    else:
        assert seq_len & (seq_len - 1) == 0 and tm % seq_len == 0
        tap_specs = [_row_spec(tm, ff), _row_spec(tm, ff)]
        tail_shape = jax.ShapeDtypeStruct((b, t, ff), F32)
        tail_spec = _row_spec(tm, ff)
        scratch = []
        sem = ("parallel", "parallel")
    return pl.pallas_call(
        kern,
        out_shape=[jax.ShapeDtypeStruct((b, t, d), F32), tail_shape],
        grid=(b, t // tm),
        in_specs=[_row_spec(tm, d), _mod_spec(sh, tm), _mod_spec(sc, tm), _mod_spec(gate, tm)]
        + tap_specs + wspecs,
        out_specs=[_row_spec(tm, d), tail_spec],
        scratch_shapes=scratch,
        compiler_params=_params(*sem),
        name="conv_ffn",
    )(x, sh, sc, gate, *taps, w_up, w_conv, b_conv, w_down, lng, lnb)


def _rope_rows(x, cos, sin, hd):
    half = hd // 2
    lane = lax.broadcasted_iota(jnp.int32, (x.shape[0], LANES), 1)
    first = (lane & (hd - 1)) < half
    out = []
    for s in range(x.shape[1] // LANES):
        xs = x[:, s * LANES:(s + 1) * LANES]
        partner = jnp.where(first, pltpu.roll(xs, LANES - half, 1), pltpu.roll(xs, half, 1))
        out.append(xs * cos + partner * sin)
    return out


def _rope_cols(x3, cos_t, sin_t):
    half = x3.shape[1] // 2
    partner = jnp.concatenate([x3[:, half:, :], x3[:, :half, :]], axis=1)
    return x3 * cos_t[None] + partner * sin_t[None]


def _kvq_decode_kernel(x_ref, shk_ref, sck_ref, shq_ref, scq_ref, wk_ref, wv_ref, wq_ref,
                       cos_ref, sin_ref, k_ref, v_ref, q_ref, *, hd):
    x = x_ref[0]
    hk = (x * (1.0 + sck_ref[0]) + shk_ref[0]).astype(BF)
    hq = (x * (1.0 + scq_ref[0]) + shq_ref[0]).astype(BF)
    cos = cos_ref[...]
    sin = sin_ref[...]
    for s, piece in enumerate(_rope_rows(_dot(hk, wk_ref[...]), cos, sin, hd)):
        k_ref[0, :, s * LANES:(s + 1) * LANES] = piece
    v_ref[0] = _dot(hk, wv_ref[...])
    for s, piece in enumerate(_rope_rows(_dot(hq, wq_ref[...]), cos, sin, hd)):
        q_ref[0, :, s * LANES:(s + 1) * LANES] = piece


def _kvq_prompt_kernel(x_ref, shk_ref, sck_ref, shq_ref, scq_ref, wk_ref, wkt_ref, wvt_ref,
                       wqt_ref, cos_ref, sin_ref, cos_t_ref, sin_t_ref,
                       kt_ref, vt_ref, krow_ref, vtb_ref, qt_ref, km_ref, *, hd, blk):
    x = x_ref[0]
    tm = x.shape[0]
    hk = (x * (1.0 + sck_ref[0]) + shk_ref[0]).astype(BF)
    hq = (x * (1.0 + scq_ref[0]) + shq_ref[0]).astype(BF)
    heads = kt_ref.shape[1]
    cos_t = cos_t_ref[...]
    sin_t = sin_t_ref[...]
    kt_ref[0] = _rope_cols(_dot_nt(wkt_ref[...], hk).reshape(heads, hd, tm), cos_t, sin_t)
    vt = _dot_nt(wvt_ref[...], hk).reshape(heads, hd, tm)
    vt_ref[0] = vt
    vtb_ref[0, :, 0:hd, :] = vt.astype(BF)
    vtb_ref[0, :, hd:, :] = jnp.ones((heads, vtb_ref.shape[2] - hd, tm), BF)
    qt_ref[0] = _rope_cols(_dot_nt(wqt_ref[...], hq).reshape(heads, hd, tm), cos_t,
                           sin_t).reshape(heads * hd, tm)
    for s, piece in enumerate(_rope_rows(_dot(hk, wk_ref[...]), cos_ref[...], sin_ref[...], hd)):
        cols = slice(s * LANES, (s + 1) * LANES)
        krow_ref[0, :, cols] = piece.astype(BF)
        for j in range(tm // blk):
            km_ref[0, 0, j:j + 1, cols] = jnp.mean(piece[j * blk:(j + 1) * blk, :], axis=0,
                                                   keepdims=True)


def _kvq_specs(x, mods, tm):
    return [_row_spec(tm, x.shape[2])] + [_mod_spec(m, tm) for m in mods]


def _kvq_decode(x, mods, wk, wv, wq, tables, hd, tm):
    b, t, d = x.shape
    dm = wq.shape[1]
    cos, sin = tables[:2]
    tab_spec = pl.BlockSpec((tm, LANES), lambda bb, tt: (tt, 0))
    row = jax.ShapeDtypeStruct((b, t, dm), F32)
    return pl.pallas_call(
        functools.partial(_kvq_decode_kernel, hd=hd),
        out_shape=[row, row, row],
        grid=(b, t // tm),
        in_specs=_kvq_specs(x, mods, tm) + [_const_spec(wk.shape), _const_spec(wv.shape),
                                            _const_spec(wq.shape), tab_spec, tab_spec],
        out_specs=[_row_spec(tm, dm)] * 3,
        compiler_params=_params("parallel", "parallel"),
        name="kvq_decode",
    )(x, *mods, wk, wv, wq, cos, sin)


def _kvq_prompt(x, mods, wk, wv, wq, tables, hd, tm):
    b, t, d = x.shape
    dm = wq.shape[1]
    heads = dm // hd
    blk = MOBA_BLOCK
    cos, sin, cos_t, sin_t = tables
    tab_spec = pl.BlockSpec((tm, LANES), lambda bb, tt: (tt, 0))
    tab_t_spec = pl.BlockSpec((hd, tm), lambda bb, tt: (0, tt))
    col_spec = pl.BlockSpec((1, heads, hd, tm), lambda bb, tt: (bb, 0, 0, tt))
    w_spec = _const_spec(wk.shape)
    return pl.pallas_call(
        functools.partial(_kvq_prompt_kernel, hd=hd, blk=blk),
        out_shape=[jax.ShapeDtypeStruct((b, heads, hd, t), F32),
                   jax.ShapeDtypeStruct((b, heads, hd, t), F32),
                   jax.ShapeDtypeStruct((b, t, dm), BF),
                   jax.ShapeDtypeStruct((b, heads, hd + SUM_ROWS, t), BF),
                   jax.ShapeDtypeStruct((b, dm, t), F32),
                   jax.ShapeDtypeStruct((b, t // tm, tm // blk, dm), F32)],
        grid=(b, t // tm),
        in_specs=_kvq_specs(x, mods, tm) + [w_spec, w_spec, w_spec, w_spec, tab_spec, tab_spec,
                                            tab_t_spec, tab_t_spec],
        out_specs=[col_spec, col_spec, _row_spec(tm, dm),
                   pl.BlockSpec((1, heads, hd + SUM_ROWS, tm), lambda bb, tt: (bb, 0, 0, tt)),
                   pl.BlockSpec((1, dm, tm), lambda bb, tt: (bb, 0, tt)),
                   pl.BlockSpec((1, 1, tm // blk, dm), lambda bb, tt: (bb, tt, 0, 0))],
        compiler_params=_params("parallel", "parallel"),
        name="kvq_prompt",
    )(x, *mods, wk, wk.T, wv.T, wq.T, cos, sin, cos_t, sin_t)


def _rope_tables(pos, hd):
    half = hd // 2
    inv = ROPE_THETA ** (-jnp.arange(half, dtype=F32) / half)
    ang = pos.astype(F32)[:, None] * inv[None, :]
    cos, sin = jnp.cos(ang), jnp.sin(ang)
    cos_h = jnp.concatenate([cos, cos], axis=1)
    sin_h = jnp.concatenate([-sin, sin], axis=1)
    reps = LANES // hd
    return jnp.tile(cos_h, (1, reps)), jnp.tile(sin_h, (1, reps)), cos_h.T, sin_h.T


def _select_bias(gate, n_valid, topk, axis):
    bidx = lax.broadcasted_iota(jnp.int32, gate.shape, axis)
    bidx_f = bidx.astype(F32)
    ninf = jnp.float32(-jnp.inf)
    gate = jnp.where(bidx < n_valid, gate, ninf)
    sel = jnp.zeros(gate.shape, jnp.bool_)
    for _ in range(topk):
        mx = jnp.max(gate, axis=axis, keepdims=True)
        first = jnp.min(jnp.where(gate == mx, bidx_f, jnp.float32(gate.shape[axis])), axis=axis,
                        keepdims=True)
        pick = (bidx_f == first) & (mx > ninf)
        sel = sel | pick
        gate = jnp.where(pick, ninf, gate)
    return jnp.where(sel, 0.0, MASKED)


def _gate_scores(q, km):
    return jnp.dot(q, km, preferred_element_type=F32, precision=lax.Precision.HIGHEST)


def _moba_prompt_kernel(qt_ref, k_ref, vt_ref, km_ref, o_ref, qb_scr, bias_scr, acc_scr,
                        *, hd, blk, kb, topk, scale):
    i = pl.program_id(2)
    n_heads = LANES // hd
    qt = qt_ref[0]
    km = km_ref[0]
    rowi = lax.broadcasted_iota(jnp.int32, qt.shape, 0)
    key = lax.broadcasted_iota(jnp.int32, (blk, blk), 0)
    qry = lax.broadcasted_iota(jnp.int32, (blk, blk), 1)
    own = pl.multiple_of(i * blk, blk)
    k_own = k_ref[0, pl.ds(own, blk), :]
    maxes = []
    for hh in range(n_heads):
        qh = jnp.where((rowi >= hh * hd) & (rowi < (hh + 1) * hd), qt, 0.0)
        bias_scr[hh] = _select_bias(_gate_scores(km, qh), i, topk, 0)
        qb = (qh * (scale * LOG2_E)).astype(BF)
        qb_scr[hh] = qb
        s = jnp.where(key <= qry, _dot(k_own, qb), MASKED)
        m = jnp.max(s, axis=0, keepdims=True)
        acc_scr[hh] = _dot(vt_ref[0, hh, :, pl.ds(own, blk)], jnp.exp2(s - m).astype(BF))
        maxes.append(m)

    def body(it, carry):
        tiles = [(hh, b, pl.multiple_of((it * kb + b) * blk, blk))
                 for b in range(kb) for hh in range(n_heads)]
        scores = [_dot(k_ref[0, pl.ds(start, blk), :], qb_scr[hh]) for hh, b, start in tiles]
        parts = [[] for _ in range(n_heads)]
        probs = []
        for (hh, b, start), s in zip(tiles, scores):
            cm = jnp.max(s, axis=0, keepdims=True)
            probs.append(jnp.exp2(s - cm).astype(BF))
            parts[hh].append([cm + bias_scr[hh, pl.ds(it * kb + b, 1), :]])
        for (hh, b, start), p in zip(tiles, probs):
            parts[hh][b].append(_dot(vt_ref[0, hh, :, pl.ds(start, blk)], p))
        new = []
        for hh in range(n_heads):
            m_prev = carry[hh]
            m_new = m_prev
            for m_b, _ in parts[hh]:
                m_new = jnp.maximum(m_new, m_b)
            acc = jnp.exp2(m_prev - m_new) * acc_scr[hh]
            for m_b, acc_b in parts[hh]:
                acc = acc + jnp.exp2(m_b - m_new) * acc_b
            acc_scr[hh] = acc
            new.append(m_new)
        return tuple(new)

    lax.fori_loop(0, (i + kb - 1) // kb, body, tuple(maxes))
    out_t = jnp.concatenate([acc_scr[hh, 0:hd, :] / acc_scr[hh, hd:hd + 1, :]
                             for hh in range(n_heads)], axis=0)
    o_ref[0] = out_t.T.astype(o_ref.dtype)


def _moba_prompt(q_t, k, v_t, kmean, hd):
    b, dm, t = q_t.shape
    blk = MOBA_BLOCK
    n_blocks = t // blk
    n_heads = LANES // hd
    kb = 4 if n_blocks % 4 == 0 else 1
    kern = functools.partial(_moba_prompt_kernel, hd=hd, blk=blk, kb=kb, topk=MOBA_TOPK,
                             scale=float(hd) ** -0.5)
    return pl.pallas_call(
        kern,
        out_shape=jax.ShapeDtypeStruct((b, t, dm), BF),
        grid=(b, dm // LANES, n_blocks),
        in_specs=[pl.BlockSpec((1, LANES, blk), lambda bb, hp, i: (bb, hp, i)),
                  pl.BlockSpec((1, t, LANES), lambda bb, hp, i: (bb, 0, hp)),
                  pl.BlockSpec((1, n_heads, v_t.shape[2], t), lambda bb, hp, i: (bb, hp, 0, 0)),
                  pl.BlockSpec((1, n_blocks, LANES), lambda bb, hp, i: (bb, 0, hp))],
        out_specs=pl.BlockSpec((1, blk, LANES), lambda bb, hp, i: (bb, i, hp)),
        scratch_shapes=[pltpu.VMEM((n_heads, LANES, blk), BF),
                        pltpu.VMEM((n_heads, n_blocks, blk), F32),
                        pltpu.VMEM((n_heads, v_t.shape[2], blk), F32)],
        compiler_params=_params("parallel", "parallel", "arbitrary"),
        name="moba_prompt",
    )(q_t, k, v_t, kmean)


def _moba_decode_kernel(*refs, hd, ts, bps, n_new, n_blocks, topk, scale):
    n_pg = 2 * bps
    pt_ref, qm_ref, kn_ref, vn_ref = refs[:4]
    k_refs = refs[4:4 + n_pg]
    v_refs = refs[4 + n_pg:4 + 2 * n_pg]
    o_ref, qb_scr, km_scr, m_scr, l_scr, acc_all = refs[4 + 2 * n_pg:]
    j = pl.program_id(1)
    rows, dm = qm_ref.shape[1], qm_ref.shape[2]
    page = k_refs[0].shape[3]
    blk = 2 * page
    col_km = lax.broadcasted_iota(jnp.int32, km_scr.shape, 1)
    col_st = lax.broadcasted_iota(jnp.int32, m_scr.shape, 1)

    @pl.when(j == 0)
    def _():
        qb_scr[...] = (qm_ref[0] * (scale * LOG2_E)).astype(BF)
        km_scr[...] = jnp.zeros_like(km_scr)
        m_scr[...] = jnp.full(m_scr.shape, MASKED, F32)
        l_scr[...] = jnp.zeros_like(l_scr)

    qb = qb_scr[...]
    for b in range(bps):
        idx = j * bps + b
        ke = k_refs[2 * b][0].reshape(dm, page)
        ko = k_refs[2 * b + 1][0].reshape(dm, page)
        mean = jnp.sum(ke + ko, axis=-1, keepdims=True) * (1.0 / blk)
        km_scr[...] = jnp.where(col_km == idx, mean, km_scr[...])
        s = jnp.concatenate([_dot(qb, ke.astype(BF)), _dot(qb, ko.astype(BF))], axis=-1)
        cm = jnp.max(s, axis=-1, keepdims=True)
        p = jnp.exp2(s - cm)
        m_scr[...] = jnp.where(col_st == idx, cm, m_scr[...])
        l_scr[...] = jnp.where(col_st == idx, jnp.sum(p, axis=-1, keepdims=True), l_scr[...])
        pb = p.astype(BF)
        acc_all[idx] = (_dot_nt(pb[:, :page], v_refs[2 * b][0].reshape(dm, page).astype(BF))
                        + _dot_nt(pb[:, page:], v_refs[2 * b + 1][0].reshape(dm, page).astype(BF)))

    @pl.when(j == pl.num_programs(1) - 1)
    def _():
        gate = _gate_scores(qm_ref[0], km_scr[...])
        m_blk = m_scr[...] + _select_bias(gate, n_blocks, topk, 1)
        s = _dot_nt(qb, kn_ref[0].astype(BF))
        qpos = lax.broadcasted_iota(jnp.int32, s.shape, 0) & (ts - 1)
        kpos = lax.broadcasted_iota(jnp.int32, s.shape, 1)
        s = jnp.where((kpos <= qpos) & (kpos < n_new), s, MASKED)
        m_own = jnp.max(s, axis=-1, keepdims=True)
        p = jnp.exp2(s - m_own)
        m_all = jnp.maximum(m_own, jnp.max(m_blk, axis=-1, keepdims=True))
        w_own = jnp.exp2(m_own - m_all)
        w_blk = jnp.exp2(m_blk - m_all)
        l_all = (w_own * jnp.sum(p, axis=-1, keepdims=True)
                 + jnp.sum(w_blk * l_scr[...], axis=-1, keepdims=True))
        acc0 = w_own * _dot(p.astype(BF), vn_ref[0].astype(BF))

        def merge(jb, acc):
            w = jnp.sum(jnp.where(col_st == jb, w_blk, 0.0), axis=-1, keepdims=True)
            return acc + w * acc_all[jb]

        res = lax.fori_loop(0, n_blocks, merge, acc0) / l_all
        row_h = lax.broadcasted_iota(jnp.int32, (rows, dm), 0) >> (ts.bit_length() - 1)
        lane_h = lax.broadcasted_iota(jnp.int32, (rows, dm), 1) >> (hd.bit_length() - 1)
        own_head = jnp.where(row_h == lane_h, res, 0.0).astype(BF)
        pick = (lax.broadcasted_iota(jnp.int32, (o_ref.shape[1], rows), 1) & (ts - 1)
                == lax.broadcasted_iota(jnp.int32, (o_ref.shape[1], rows), 0)).astype(BF)
        o_ref[0] = _dot(pick, own_head).astype(o_ref.dtype)


def _moba_decode(page_table, qm, k_new, v_new, cache_k_t, cache_v_t, ts, qpad):
    s, rows, dm = qm.shape
    _, n_heads, hd, page = cache_k_t.shape
    n_blocks = page_table.shape[1] // 2
    bps = 2 if n_blocks % 2 == 0 else 1
    assert 2 * page == MOBA_BLOCK and n_blocks <= LANES and rows == n_heads * ts
    kern = functools.partial(_moba_decode_kernel, hd=hd, ts=ts, bps=bps, n_new=ts,
                             n_blocks=n_blocks, topk=MOBA_TOPK, scale=float(hd) ** -0.5)

    def page_spec(k):
        return pl.BlockSpec((1, n_heads, hd, page),
                            lambda ss, j, pt: (pt[ss, 2 * bps * j + k], 0, 0, 0))

    pages = [page_spec(k) for k in range(2 * bps)]
    per_seq = lambda a: pl.BlockSpec((1,) + a.shape[1:], lambda ss, j, pt: (ss, 0, 0))
    grid_spec = pltpu.PrefetchScalarGridSpec(
        num_scalar_prefetch=1,
        grid=(s, n_blocks // bps),
        in_specs=[per_seq(qm), per_seq(k_new), per_seq(v_new)] + pages + pages,
        out_specs=pl.BlockSpec((1, qpad, dm), lambda ss, j, pt: (ss, 0, 0)),
        scratch_shapes=[pltpu.VMEM((rows, dm), BF),
                        pltpu.VMEM((dm, LANES), F32),
                        pltpu.VMEM((rows, LANES), F32),
                        pltpu.VMEM((rows, LANES), F32),
                        pltpu.VMEM((n_blocks, rows, dm), F32)],
    )
    return pl.pallas_call(
        kern,
        out_shape=jax.ShapeDtypeStruct((s, qpad, dm), BF),
        grid_spec=grid_spec,
        compiler_params=_params("parallel", "arbitrary"),
        name="moba_decode",
    )(page_table, qm, k_new, v_new, *([cache_k_t] * (2 * bps)), *([cache_v_t] * (2 * bps)))


def _pad_rows(a, n, axis=1):
    pad = [(0, 0)] * a.ndim
    pad[axis] = (0, n - a.shape[axis])
    return jnp.pad(a, pad)


def _trunk(x, mods, kv_mod, gla_fns, conv_taps, seq_len, kvq_fn, attend, w, tiles):
    depth = len(mods)
    n_a = depth // 2
    alpha = float((2 * depth) ** 0.25)
    heads, dk, dv = w["gla_dims"]
    gla_states, conv_tails = [], []
    k_sh = v_sh = ctx = None
    for i in range(depth):
        sh_m, sc_m, g_m, sh_f, sc_f, g_f = mods[i]
        if i < n_a:
            q, k, v, g, gk = _gla_proj(x, sh_m, sc_m, w["a_w_in"][i], w["a_w_gk1"][i],
                                       w["a_w_gk2"][i], w["a_b_gk"][i], heads * dk, heads * dv,
                                       float(dk) ** -0.5, tiles["proj"])
            o, s_fin = gla_fns[i](q, k, v, g, gk)
            gla_states.append(s_fin)
            w_out = w["a_w_out"][i]
        else:
            o = attend(ctx)
            w_out = w["b_w_out"][i - n_a]
        x = _out_ln(x, o, g_m, w_out, w["ln_g"][i, 0:1], w["ln_b"][i, 0:1], alpha, tiles["row"])
        x, tail = _ffn(x, sh_f, sc_f, g_f, conv_taps[i], w["w_up"][i], w["w_conv"][i],
                       w["b_conv"][i:i + 1], w["w_down"][i], w["ln_g"][i, 1:2], w["ln_b"][i, 1:2],
                       alpha, tiles["ffn"], seq_len)
        conv_tails.append(tail)
        if i == n_a - 1:
            k_sh, v_sh, ctx = kvq_fn(x, kv_mod[0], kv_mod[1], mods[n_a][0], mods[n_a][1])
    return x, k_sh, v_sh, gla_states, conv_tails


def kernel(x_prompt, x_sample, cache_k, cache_v, state_gla, state_ffn_conv, page_table, c_prompt,
           c_sample, w_ada, b_ada, ln_g, ln_b, w_up, w_conv, b_conv, w_down, a_w_in, a_w_gk1,
           a_w_gk2, a_b_gk, a_norm_g, a_w_out, kv_w_ada, kv_b_ada, kv_w, b_w_q, b_w_out):
    bp, tp, d = x_prompt.shape
    bs, ts, _ = x_sample.shape
    depth = w_ada.shape[0]
    n_a = state_gla.shape[0]
    heads, dk, dv = state_gla.shape[2:]
    n_phys, page, m_heads, hd = cache_k.shape
    dm = m_heads * hd
    n_pages = page_table.shape[1]
    past_len = n_pages * page
    ff = w_down.shape[1]
    blk = MOBA_BLOCK
    assert depth == 2 * n_a == 2 and b_w_q.shape[0] == 1
    assert tp % blk == 0 and past_len % blk == 0
    assert 2 <= ts <= 8 and ts & (ts - 1) == 0
    assert LANES % hd == 0 and dm % LANES == 0

    w_k, w_v = kv_w[:, :dm].astype(BF), kv_w[:, dm:].astype(BF)
    w_q = b_w_q[0].astype(BF)
    w = {
        "gla_dims": (heads, dk, dv),
        "a_w_in": a_w_in.astype(BF),
        "a_w_gk1": _pad_rows(a_w_gk1, GLA_GATE_RANK_PAD, axis=2).astype(BF),
        "a_w_gk2": _pad_rows(a_w_gk2, GLA_GATE_RANK_PAD, axis=1).astype(BF),
        "a_b_gk": a_b_gk[:, None, :], "a_w_out": a_w_out.astype(BF),
        "w_up": w_up.astype(BF), "w_conv": w_conv, "b_conv": b_conv, "w_down": w_down.astype(BF),
        "ln_g": ln_g, "ln_b": ln_b, "b_w_out": b_w_out.astype(BF),
    }
    norm_g = [a_norm_g[i][None, :] for i in range(n_a)]

    n_c = bp + bs
    c_all = _pad_rows(jnp.concatenate([c_prompt, c_sample], axis=0), -(-n_c // 8) * 8, axis=0)
    mod_all = [_ada(c_all, w_ada, i, b_ada[i:i + 1]) for i in range(depth)]
    kv_mod_all = _ada(c_all, kv_w_ada[None], 0, kv_b_ada[None, :])

    def per_batch(m, n):
        return [m[:bp, j * d:(j + 1) * d][:, None, :] for j in range(n)]

    def per_token(m, n):
        return [jnp.repeat(m[bp:bp + bs, j * d:(j + 1) * d], ts, axis=0)[None] for j in range(n)]

    def gla_prompt(i):
        s0 = jnp.zeros((bp, heads, dv, dk), F32)
        return lambda q, k, v, g, gk: _gla_rec(q, k, v, g, gk, s0, norm_g[i], heads, dk, dv, 256)

    tables_p = _rope_tables(jnp.arange(tp, dtype=jnp.int32), hd)
    tiles_p = {"proj": 512, "row": 512, "ffn": 512}

    def kvq_prompt(x, *mods):
        k_t, v_t, k_row, v_tb, q_t, kmean = _kvq_prompt(x, mods, w_k, w_v, w_q, tables_p, hd,
                                                        tiles_p["row"])
        return k_t, v_t, (q_t, k_row, v_tb, kmean.reshape(bp, tp // blk, dm))

    zero_taps = [[jnp.zeros((bp, 2, ff), F32)] for _ in range(depth)]
    y_p, k_p, v_p, gla_p, conv_p = _trunk(
        x_prompt, [per_batch(m, 6) for m in mod_all], per_batch(kv_mod_all, 2),
        [gla_prompt(i) for i in range(n_a)], zero_taps, None, kvq_prompt,
        lambda ctx: _moba_prompt(*ctx, hd), w, tiles_p)

    n_rows = bs * ts
    xs = x_sample.reshape(1, n_rows, d)
    pos_s = past_len + jnp.arange(ts, dtype=jnp.int32)

    def gla_sample(i):
        s0 = jnp.swapaxes(state_gla[i], -1, -2)

        def run(q, k, v, g, gk):
            def seqs(a):
                return _pad_rows(a.reshape(bs, ts, a.shape[-1]), GLA_CHUNK)
            o, s_fin = _gla_rec(seqs(q), seqs(k), seqs(v), seqs(g), seqs(gk), s0, norm_g[i],
                                heads, dk, dv, GLA_CHUNK)
            return o[:, :ts].reshape(1, n_rows, heads * dv), s_fin
        return run

    qpad = 8
    cache_k_t = jnp.transpose(cache_k, (0, 2, 3, 1))
    cache_v_t = jnp.transpose(cache_v, (0, 2, 3, 1))
    tables_s = _rope_tables(jnp.tile(pos_s, bs), hd)
    tiles_s = {"proj": n_rows, "row": n_rows, "ffn": n_rows}

    def kvq_sample(x, *mods):
        k, v, q = _kvq_decode(x, mods, w_k, w_v, w_q, tables_s, hd, tiles_s["row"])
        return k, v, (q, k, v)

    def attend_sample(ctx):
        q, k, v = ctx
        seqs = lambda a, n: _pad_rows(a.reshape(bs, ts, dm), n)
        head_mask = (jnp.arange(dm)[None, :] // hd == jnp.arange(m_heads)[:, None]).astype(F32)
        qm = (q.reshape(bs, 1, ts, dm) * head_mask[None, :, None, :]).reshape(bs, m_heads * ts, dm)
        o = _moba_decode(page_table, qm, seqs(k, LANES), seqs(v, LANES), cache_k_t, cache_v_t,
                         ts, qpad)
        return o[:, :ts].reshape(1, n_rows, dm)

    def taps(i):
        st = state_ffn_conv[i]
        zeros = jnp.zeros((bs, ts - 1, ff), F32)
        tap1 = jnp.concatenate([st[:, 1:2], zeros], axis=1)
        tap2 = jnp.concatenate([st[:, 0:1], st[:, 1:2], zeros[:, :ts - 2]], axis=1)
        return [tap1.reshape(1, n_rows, ff), tap2.reshape(1, n_rows, ff)]

    y_s, k_s, v_s, gla_s, conv_s = _trunk(
        xs, [per_token(m, 6) for m in mod_all], per_token(kv_mod_all, 2),
        [gla_sample(i) for i in range(n_a)], [taps(i) for i in range(depth)], ts, kvq_sample,
        attend_sample, w, tiles_s)

    gla_state_p = jnp.stack([jnp.swapaxes(s, -1, -2) for s in gla_p])
    gla_state_s = jnp.stack([jnp.swapaxes(s, -1, -2) for s in gla_s])
    conv_state_p = jnp.stack(conv_p)
    conv_state_s = jnp.stack([g.reshape(bs, ts, ff)[:, ts - 2:] for g in conv_s])
    return (y_p, y_s.reshape(bs, ts, d),
            jnp.transpose(k_p, (0, 3, 1, 2)), jnp.transpose(v_p, (0, 3, 1, 2)),
            k_s.reshape(bs, ts, m_heads, hd), v_s.reshape(bs, ts, m_heads, hd),
            gla_state_p, gla_state_s, conv_state_p, conv_state_s)
```

```python
import functools

import jax
import jax.numpy as jnp
from jax import lax
from jax.experimental import pallas as pl
from jax.experimental.pallas import tpu as pltpu

F32 = jnp.float32
BF = jnp.bfloat16

LN_EPS = 1e-5
GLA_CHUNK = 64
GLA_GATE_NORM = 16.0
GLA_GATE_RANK_PAD = 128
MOBA_BLOCK = 256
MOBA_TOPK = 3
ROPE_THETA = 10000.0
MASKED = -1e30
LOG2_E = 1.4426950408889634
SUM_ROWS = 16
LANES = 128
VMEM_LIMIT = 56 * 1024 * 1024


def _params(*sem):
    return pltpu.CompilerParams(dimension_semantics=sem, vmem_limit_bytes=VMEM_LIMIT)


def _dot(a, b):
    return jnp.dot(a, b, preferred_element_type=F32)


def _dot_nt(a, b):
    return lax.dot_general(a, b, (((1,), (1,)), ((), ())), preferred_element_type=F32)


def _dot_tn(a, b):
    return lax.dot_general(a, b, (((0,), (0,)), ((), ())), preferred_element_type=F32)


def _layer_norm(z, g, b):
    mu = jnp.mean(z, axis=-1, keepdims=True)
    zc = z - mu
    var = jnp.mean(zc * zc, axis=-1, keepdims=True)
    return zc * lax.rsqrt(var + LN_EPS) * g + b


def _const_spec(shape):
    nd = len(shape)
    return pl.BlockSpec(shape, lambda *_: (0,) * nd, pipeline_mode=pl.Buffered(1))


def _mod_spec(arr, tm):
    if arr.shape[1] == 1:
        return pl.BlockSpec((1, 1, arr.shape[2]), lambda b, t: (b, 0, 0))
    return pl.BlockSpec((1, tm, arr.shape[2]), lambda b, t: (b, t, 0))


def _row_spec(tm, width):
    return pl.BlockSpec((1, tm, width), lambda b, t: (b, t, 0))


def _ada_kernel(c_ref, w_ref, b_ref, o_ref):
    c = c_ref[...]
    s = c * jax.nn.sigmoid(c)
    o_ref[...] = _dot(s.astype(BF), w_ref[...].astype(BF)) + b_ref[...]


def _ada(c, w3, layer, b2, tn=1024):
    r, d = c.shape
    n = w3.shape[2]
    return pl.pallas_call(
        _ada_kernel,
        out_shape=jax.ShapeDtypeStruct((r, n), F32),
        grid=(n // tn,),
        in_specs=[pl.BlockSpec((r, d), lambda j: (0, 0)),
                  pl.BlockSpec((None, d, tn), lambda j: (layer, 0, j)),
                  pl.BlockSpec((1, tn), lambda j: (0, j))],
        out_specs=pl.BlockSpec((r, tn), lambda j: (0, j)),
        compiler_params=_params("arbitrary"),
        name="ada_mod",
    )(c, w3, b2)


def _gla_proj_kernel(x_ref, sh_ref, sc_ref, win_ref, wg1_ref, wg2_ref, bgk_ref,
                     q_ref, k_ref, v_ref, g_ref, gk_ref, *, kd, vd, scale):
    h = (x_ref[0] * (1.0 + sc_ref[0]) + sh_ref[0]).astype(BF)
    q_ref[0] = _dot(h, win_ref[:, 0:kd]) * scale
    k_ref[0] = _dot(h, win_ref[:, kd:2 * kd])
    v_ref[0] = _dot(h, win_ref[:, 2 * kd:2 * kd + vd]).astype(BF)
    g_ref[0] = _dot(h, win_ref[:, 2 * kd + vd:2 * kd + 2 * vd])
    r = _dot(h, wg1_ref[...])
    lin = _dot(r.astype(BF), wg2_ref[...]) + bgk_ref[...]
    gk_ref[0] = (jnp.minimum(lin, 0.0) - jnp.log1p(jnp.exp(-jnp.abs(lin)))) * (1.0 / GLA_GATE_NORM)


def _gla_proj(x, sh, sc, w_in, wg1, wg2, bgk, kd, vd, scale, tm):
    b, t, d = x.shape
    kern = functools.partial(_gla_proj_kernel, kd=kd, vd=vd, scale=scale)
    return pl.pallas_call(
        kern,
        out_shape=[jax.ShapeDtypeStruct((b, t, kd), F32), jax.ShapeDtypeStruct((b, t, kd), F32),
                   jax.ShapeDtypeStruct((b, t, vd), BF), jax.ShapeDtypeStruct((b, t, vd), F32),
                   jax.ShapeDtypeStruct((b, t, kd), F32)],
        grid=(b, t // tm),
        in_specs=[_row_spec(tm, d), _mod_spec(sh, tm), _mod_spec(sc, tm),
                  _const_spec(w_in.shape), _const_spec(wg1.shape), _const_spec(wg2.shape),
                  _const_spec(bgk.shape)],
        out_specs=[_row_spec(tm, kd), _row_spec(tm, kd), _row_spec(tm, vd), _row_spec(tm, vd),
                   _row_spec(tm, kd)],
        compiler_params=_params("parallel", "parallel"),
        name="gla_proj",
    )(x, sh, sc, w_in, wg1, wg2, bgk)


def _split3(x):
    hi = x.astype(BF)
    r = x - hi.astype(F32)
    mid = r.astype(BF)
    lo = (r - mid.astype(F32)).astype(BF)
    return hi, mid, lo


def _gla_rec_kernel(q_ref, k_ref, v_ref, g_ref, gk_ref, s0_ref, ng_ref, o_ref, sfin_ref, s_scr,
                    *, heads, dk, dv, chunk, n_chunks):
    t = pl.program_id(1)

    @pl.when(t == 0)
    def _():
        s_scr[...] = s0_ref[0]

    row = lax.broadcasted_iota(jnp.int32, (chunk, chunk), 0)
    col = lax.broadcasted_iota(jnp.int32, (chunk, chunk), 1)
    tril_f = (row >= col).astype(F32)
    tril = tril_f.astype(BF)
    ng = ng_ref[...]

    for c in range(n_chunks):
        rows = pl.ds(c * chunk, chunk)
        hi, mid, lo = _split3(gk_ref[0, rows, :])
        cum = _dot(tril, hi) + _dot(tril, mid) + _dot(tril, lo)
        last = cum[chunk - 1:chunk, :]
        q_in = (q_ref[0, rows, :] * jnp.exp(cum)).astype(BF)
        kc = k_ref[0, rows, :]
        k_in = (kc * jnp.exp(-cum)).astype(BF)
        k_end = (kc * jnp.exp(last - cum)).astype(BF)
        dec = jnp.exp(last)
        for h in range(heads):
            ks = slice(h * dk, (h + 1) * dk)
            vs = slice(h * dv, (h + 1) * dv)
            a = _dot_nt(q_in[:, ks], k_in[:, ks]) * tril_f
            vh = v_ref[0, rows, vs]
            st = s_scr[h]
            o = _dot(a.astype(BF), vh) + _dot_nt(q_in[:, ks], st.astype(BF))
            s_scr[h] = st * dec[:, ks] + _dot_tn(vh, k_end[:, ks])
            o = o * lax.rsqrt(jnp.mean(o * o, axis=-1, keepdims=True) + LN_EPS) * ng
            gh = g_ref[0, rows, vs]
            o_ref[0, rows, vs] = (o * (gh * jax.nn.sigmoid(gh))).astype(BF)

    @pl.when(t == pl.num_programs(1) - 1)
    def _():
        sfin_ref[0] = s_scr[...]


def _gla_rec(q, k, v, g, gk, s0t, norm_g, heads, dk, dv, tm):
    b, t, kd = q.shape
    vd = v.shape[2]
    chunk = GLA_CHUNK
    kern = functools.partial(_gla_rec_kernel, heads=heads, dk=dk, dv=dv, chunk=chunk,
                             n_chunks=tm // chunk)
    st_spec = pl.BlockSpec((1, heads, dv, dk), lambda bb, tt: (bb, 0, 0, 0))
    return pl.pallas_call(
        kern,
        out_shape=[jax.ShapeDtypeStruct((b, t, vd), BF),
                   jax.ShapeDtypeStruct((b, heads, dv, dk), F32)],
        grid=(b, t // tm),
        in_specs=[_row_spec(tm, kd), _row_spec(tm, kd), _row_spec(tm, vd), _row_spec(tm, vd),
                  _row_spec(tm, kd), st_spec, _const_spec(norm_g.shape)],
        out_specs=[_row_spec(tm, vd), st_spec],
        scratch_shapes=[pltpu.VMEM((heads, dv, dk), F32)],
        compiler_params=_params("parallel", "arbitrary"),
        name="gla_rec",
    )(q, k, v, g, gk, s0t, norm_g)


def _ffn_kernel(*refs, alpha, ff, fc, tm, seq_len):
    x_ref, o_ref, gm_ref, wout_ref, lng0_ref, lnb0_ref, sh_ref, sc_ref, gate_ref = refs[:9]
    if seq_len is None:
        (st_ref, wup_ref, wconv_ref, bconv_ref, wdown_ref,
         lng_ref, lnb_ref, out_ref, tail_ref, act, carry) = refs[9:]
    else:
        (tap1_ref, tap2_ref, wup_ref, wconv_ref, bconv_ref,
         wdown_ref, lng_ref, lnb_ref, out_ref, tail_ref, act) = refs[9:]
    t = pl.program_id(1)
    x = _layer_norm(alpha * x_ref[0] + (1.0 + gm_ref[0]) * _dot(o_ref[0], wout_ref[...]),
                    lng0_ref[...], lnb0_ref[...])
    h = (x * (1.0 + sc_ref[0]) + sh_ref[0]).astype(BF)
    rowi = lax.broadcasted_iota(jnp.int32, (tm, fc), 0)
    for c in range(ff // fc):
        cs = slice(c * fc, (c + 1) * fc)
        u = _dot(h, wup_ref[:, c * fc:(c + 1) * fc])
        g = _dot(h, wup_ref[:, ff + c * fc:ff + (c + 1) * fc])
        g1 = pltpu.roll(g, 1, 0)
        g2 = pltpu.roll(g, 2, 0)
        if seq_len is None:
            prev = jnp.where(t == 0, st_ref[0, :, cs], carry[0:2, cs])
            g1 = jnp.where(rowi == 0, prev[1:2, :], g1)
            g2 = jnp.where(rowi == 0, prev[0:1, :], jnp.where(rowi == 1, prev[1:2, :], g2))
            carry[0:2, cs] = g[tm - 2:tm, :]
            tail_ref[0, :, cs] = g[tm - 2:tm, :]
        else:
            pos = rowi & (seq_len - 1)
            g1 = jnp.where(pos >= 1, g1, tap1_ref[0, :, cs])
            g2 = jnp.where(pos >= 2, g2, tap2_ref[0, :, cs])
            tail_ref[0, :, cs] = g
        gc = (bconv_ref[:, cs] + g2 * wconv_ref[0:1, cs] + g1 * wconv_ref[1:2, cs]
              + g * wconv_ref[2:3, cs])
        act[:, cs] = (0.5 * gc * (1.0 + lax.erf(gc * 0.7071067811865476)) * u).astype(BF)
    z = alpha * x + (1.0 + gate_ref[0]) * _dot(act[...], wdown_ref[...])
    out_ref[0] = _layer_norm(z, lng_ref[...], lnb_ref[...])


def _ffn(x, o, gate_m, w_out, lng0, lnb0, sh, sc, gate, taps, w_up, w_conv, b_conv, w_down, lng,
         lnb, alpha, tm, seq_len):
    b, t, d = x.shape
    ff = w_down.shape[0]
    fc = 256 if ff % 256 == 0 else LANES
    kern = functools.partial(_ffn_kernel, alpha=alpha, ff=ff, fc=fc, tm=tm, seq_len=seq_len)
    wspecs = [_const_spec(w_up.shape), _const_spec(w_conv.shape), _const_spec(b_conv.shape),
              _const_spec(w_down.shape), _const_spec(lng.shape), _const_spec(lnb.shape)]
    mixer_specs = [_row_spec(tm, d), _row_spec(tm, o.shape[2]), _mod_spec(gate_m, tm),
                   _const_spec(w_out.shape), _const_spec(lng0.shape), _const_spec(lnb0.shape)]
    if seq_len is None:
        tap_specs = [pl.BlockSpec((1, 2, ff), lambda bb, tt: (bb, 0, 0))]
        tail_shape = jax.ShapeDtypeStruct((b, 2, ff), F32)
        tail_spec = pl.BlockSpec((1, 2, ff), lambda bb, tt: (bb, 0, 0))
        scratch = [pltpu.VMEM((tm, ff), BF), pltpu.VMEM((8, ff), F32)]
        sem = ("parallel", "arbitrary")
    else:
        assert seq_len & (seq_len - 1) == 0 and tm % seq_len == 0
        tap_specs = [_row_spec(tm, ff), _row_spec(tm, ff)]
        tail_shape = jax.ShapeDtypeStruct((b, t, ff), F32)
        tail_spec = _row_spec(tm, ff)
        scratch = [pltpu.VMEM((tm, ff), BF)]
        sem = ("parallel", "parallel")
    return pl.pallas_call(
        kern,
        out_shape=[jax.ShapeDtypeStruct((b, t, d), F32), tail_shape],
        grid=(b, t // tm),
        in_specs=mixer_specs + [_mod_spec(sh, tm), _mod_spec(sc, tm), _mod_spec(gate, tm)]
        + tap_specs + wspecs,
        out_specs=[_row_spec(tm, d), tail_spec],
        scratch_shapes=scratch,
        compiler_params=_params(*sem),
        name="conv_ffn",
    )(x, o, gate_m, w_out, lng0, lnb0, sh, sc, gate, *taps, w_up, w_conv, b_conv, w_down, lng, lnb)


def _rope_rows(x, cos, sin, hd):
    half = hd // 2
    lane = lax.broadcasted_iota(jnp.int32, (x.shape[0], LANES), 1)
    first = (lane & (hd - 1)) < half
    out = []
    for s in range(x.shape[1] // LANES):
        xs = x[:, s * LANES:(s + 1) * LANES]
        partner = jnp.where(first, pltpu.roll(xs, LANES - half, 1), pltpu.roll(xs, half, 1))
        out.append(xs * cos + partner * sin)
    return out


def _rope_cols(x3, cos_t, sin_t):
    half = x3.shape[1] // 2
    partner = jnp.concatenate([x3[:, half:, :], x3[:, :half, :]], axis=1)
    return x3 * cos_t[None] + partner * sin_t[None]


def _kvq_decode_kernel(x_ref, shk_ref, sck_ref, shq_ref, scq_ref, wk_ref, wv_ref, wq_ref,
                       cos_ref, sin_ref, k_ref, v_ref, q_ref, *, hd):
    x = x_ref[0]
    hk = (x * (1.0 + sck_ref[0]) + shk_ref[0]).astype(BF)
    hq = (x * (1.0 + scq_ref[0]) + shq_ref[0]).astype(BF)
    cos = cos_ref[...]
    sin = sin_ref[...]
    for s, piece in enumerate(_rope_rows(_dot(hk, wk_ref[...]), cos, sin, hd)):
        k_ref[0, :, s * LANES:(s + 1) * LANES] = piece
    v_ref[0] = _dot(hk, wv_ref[...])
    for s, piece in enumerate(_rope_rows(_dot(hq, wq_ref[...]), cos, sin, hd)):
        q_ref[0, :, s * LANES:(s + 1) * LANES] = piece


def _kvq_prompt_kernel(x_ref, shk_ref, sck_ref, shq_ref, scq_ref, wk_ref, wkt_ref, wvt_ref,
                       wqt_ref, cos_ref, sin_ref, cos_t_ref, sin_t_ref,
                       kt_ref, vt_ref, krow_ref, vtb_ref, qt_ref, km_ref, *, hd, blk):
    x = x_ref[0]
    tm = x.shape[0]
    hk = (x * (1.0 + sck_ref[0]) + shk_ref[0]).astype(BF)
    hq = (x * (1.0 + scq_ref[0]) + shq_ref[0]).astype(BF)
    heads = kt_ref.shape[1]
    cos_t = cos_t_ref[...]
    sin_t = sin_t_ref[...]
    kt_ref[0] = _rope_cols(_dot_nt(wkt_ref[...], hk).reshape(heads, hd, tm), cos_t, sin_t)
    vt = _dot_nt(wvt_ref[...], hk).reshape(heads, hd, tm)
    vt_ref[0] = vt
    vtb_ref[0, :, 0:hd, :] = vt.astype(BF)
    vtb_ref[0, :, hd:, :] = jnp.ones((heads, vtb_ref.shape[2] - hd, tm), BF)
    qt_ref[0] = _rope_cols(_dot_nt(wqt_ref[...], hq).reshape(heads, hd, tm), cos_t,
                           sin_t).reshape(heads * hd, tm)
    for s, piece in enumerate(_rope_rows(_dot(hk, wk_ref[...]), cos_ref[...], sin_ref[...], hd)):
        cols = slice(s * LANES, (s + 1) * LANES)
        krow_ref[0, :, cols] = piece.astype(BF)
        for j in range(tm // blk):
            km_ref[0, 0, j:j + 1, cols] = jnp.mean(piece[j * blk:(j + 1) * blk, :], axis=0,
                                                   keepdims=True)


def _kvq_specs(x, mods, tm):
    return [_row_spec(tm, x.shape[2])] + [_mod_spec(m, tm) for m in mods]


def _kvq_decode(x, mods, wk, wv, wq, tables, hd, tm):
    b, t, d = x.shape
    dm = wq.shape[1]
    cos, sin = tables[:2]
    tab_spec = pl.BlockSpec((tm, LANES), lambda bb, tt: (tt, 0))
    row = jax.ShapeDtypeStruct((b, t, dm), F32)
    return pl.pallas_call(
        functools.partial(_kvq_decode_kernel, hd=hd),
        out_shape=[row, row, row],
        grid=(b, t // tm),
        in_specs=_kvq_specs(x, mods, tm) + [_const_spec(wk.shape), _const_spec(wv.shape),
                                            _const_spec(wq.shape), tab_spec, tab_spec],
        out_specs=[_row_spec(tm, dm)] * 3,
        compiler_params=_params("parallel", "parallel"),
        name="kvq_decode",
    )(x, *mods, wk, wv, wq, cos, sin)


def _kvq_prompt(x, mods, wk, wv, wq, tables, hd, tm):
    b, t, d = x.shape
    dm = wq.shape[1]
    heads = dm // hd
    blk = MOBA_BLOCK
    cos, sin, cos_t, sin_t = tables
    tab_spec = pl.BlockSpec((tm, LANES), lambda bb, tt: (tt, 0))
    tab_t_spec = pl.BlockSpec((hd, tm), lambda bb, tt: (0, tt))
    col_spec = pl.BlockSpec((1, heads, hd, tm), lambda bb, tt: (bb, 0, 0, tt))
    w_spec = _const_spec(wk.shape)
    return pl.pallas_call(
        functools.partial(_kvq_prompt_kernel, hd=hd, blk=blk),
        out_shape=[jax.ShapeDtypeStruct((b, heads, hd, t), F32),
                   jax.ShapeDtypeStruct((b, heads, hd, t), F32),
                   jax.ShapeDtypeStruct((b, t, dm), BF),
                   jax.ShapeDtypeStruct((b, heads, hd + SUM_ROWS, t), BF),
                   jax.ShapeDtypeStruct((b, dm, t), F32),
                   jax.ShapeDtypeStruct((b, t // tm, tm // blk, dm), F32)],
        grid=(b, t // tm),
        in_specs=_kvq_specs(x, mods, tm) + [w_spec, w_spec, w_spec, w_spec, tab_spec, tab_spec,
                                            tab_t_spec, tab_t_spec],
        out_specs=[col_spec, col_spec, _row_spec(tm, dm),
                   pl.BlockSpec((1, heads, hd + SUM_ROWS, tm), lambda bb, tt: (bb, 0, 0, tt)),
                   pl.BlockSpec((1, dm, tm), lambda bb, tt: (bb, 0, tt)),
                   pl.BlockSpec((1, 1, tm // blk, dm), lambda bb, tt: (bb, tt, 0, 0))],
        compiler_params=_params("parallel", "parallel"),
        name="kvq_prompt",
    )(x, *mods, wk, wk.T, wv.T, wq.T, cos, sin, cos_t, sin_t)


def _rope_tables(pos, hd):
    half = hd // 2
    inv = ROPE_THETA ** (-jnp.arange(half, dtype=F32) / half)
    ang = pos.astype(F32)[:, None] * inv[None, :]
    cos, sin = jnp.cos(ang), jnp.sin(ang)
    cos_h = jnp.concatenate([cos, cos], axis=1)
    sin_h = jnp.concatenate([-sin, sin], axis=1)
    reps = LANES // hd
    return jnp.tile(cos_h, (1, reps)), jnp.tile(sin_h, (1, reps)), cos_h.T, sin_h.T


def _select_bias(gate, n_valid, topk, axis):
    bidx = lax.broadcasted_iota(jnp.int32, gate.shape, axis)
    bidx_f = bidx.astype(F32)
    ninf = jnp.float32(-jnp.inf)
    gate = jnp.where(bidx < n_valid, gate, ninf)
    sel = jnp.zeros(gate.shape, jnp.bool_)
    for _ in range(topk):
        mx = jnp.max(gate, axis=axis, keepdims=True)
        first = jnp.min(jnp.where(gate == mx, bidx_f, jnp.float32(gate.shape[axis])), axis=axis,
                        keepdims=True)
        pick = (bidx_f == first) & (mx > ninf)
        sel = sel | pick
        gate = jnp.where(pick, ninf, gate)
    return jnp.where(sel, 0.0, MASKED)


def _gate_scores(q, km):
    return jnp.dot(q, km, preferred_element_type=F32, precision=lax.Precision.HIGHEST)


def _head_rows(qt, hh, hd):
    rowi = lax.broadcasted_iota(jnp.int32, qt.shape, 0)
    return jnp.where((rowi >= hh * hd) & (rowi < (hh + 1) * hd), qt, 0.0)


def _moba_gate_kernel(qt_ref, km_ref, bias_ref, *, hd, blk, topk):
    tq = qt_ref.shape[2]
    qt = qt_ref[0]
    km = km_ref[0]
    pos = pl.program_id(2) * tq + lax.broadcasted_iota(jnp.int32, (1, tq), 1)
    own = pos >> (blk.bit_length() - 1)
    for hh in range(LANES // hd):
        gate = _gate_scores(km, _head_rows(qt, hh, hd))
        bias_ref[0, hh] = _select_bias(gate, own, topk, 0)


def _moba_gate(q_t, kmean, hd, tq):
    b, dm, t = q_t.shape
    n_blocks = kmean.shape[1]
    n_heads = LANES // hd
    return pl.pallas_call(
        functools.partial(_moba_gate_kernel, hd=hd, blk=MOBA_BLOCK, topk=MOBA_TOPK),
        out_shape=jax.ShapeDtypeStruct((b, dm // hd, n_blocks, t), F32),
        grid=(b, dm // LANES, t // tq),
        in_specs=[pl.BlockSpec((1, LANES, tq), lambda bb, hp, tt: (bb, hp, tt)),
                  pl.BlockSpec((1, n_blocks, LANES), lambda bb, hp, tt: (bb, 0, hp))],
        out_specs=pl.BlockSpec((1, n_heads, n_blocks, tq), lambda bb, hp, tt: (bb, hp, 0, tt)),
        compiler_params=_params("parallel", "parallel", "parallel"),
        name="moba_gate",
    )(q_t, kmean)


def _moba_prompt_kernel(qt_ref, k_ref, vt_ref, bias_ref, o_ref, qb_scr, acc_scr,
                        *, hd, blk, kb, scale):
    i = pl.program_id(2)
    per = LANES // hd
    n_heads = vt_ref.shape[1]
    heads = range(n_heads)
    key = lax.broadcasted_iota(jnp.int32, (blk, blk), 0)
    qry = lax.broadcasted_iota(jnp.int32, (blk, blk), 1)
    own = pl.multiple_of(i * blk, blk)

    def k_tile(hh, start):
        g = hh // per
        return k_ref[0, pl.ds(start, blk), g * LANES:(g + 1) * LANES]

    for hh in heads:
        g = hh // per
        qh = _head_rows(qt_ref[0, g * LANES:(g + 1) * LANES, :], hh % per, hd)
        qb_scr[hh] = (qh * (scale * LOG2_E)).astype(BF)
    own_s = [jnp.where(key <= qry, _dot(k_tile(hh, own), qb_scr[hh]), MASKED) for hh in heads]
    maxes = [jnp.max(s, axis=0, keepdims=True) for s in own_s]
    own_p = [jnp.exp2(s - m).astype(BF) for s, m in zip(own_s, maxes)]
    for hh in heads:
        acc_scr[hh] = _dot(vt_ref[0, hh, :, pl.ds(own, blk)], own_p[hh])

    def body(it, carry):
        tiles = [(hh, b, pl.multiple_of((it * kb + b) * blk, blk))
                 for b in range(kb) for hh in heads]
        scores = [_dot(k_tile(hh, start), qb_scr[hh]) for hh, b, start in tiles]
        parts = [[] for _ in heads]
        probs = []
        for (hh, b, start), s in zip(tiles, scores):
            cm = jnp.max(s, axis=0, keepdims=True)
            probs.append(jnp.exp2(s - cm).astype(BF))
            parts[hh].append([cm + bias_ref[0, hh, pl.ds(it * kb + b, 1), :]])
        for (hh, b, start), p in zip(tiles, probs):
            parts[hh][b].append(_dot(vt_ref[0, hh, :, pl.ds(start, blk)], p))
        new = []
        for hh in heads:
            m_prev = carry[hh]
            m_new = m_prev
            for m_b, _ in parts[hh]:
                m_new = jnp.maximum(m_new, m_b)
            acc = jnp.exp2(m_prev - m_new) * acc_scr[hh]
            for m_b, acc_b in parts[hh]:
                acc = acc + jnp.exp2(m_b - m_new) * acc_b
            acc_scr[hh] = acc
            new.append(m_new)
        return tuple(new)

    lax.fori_loop(0, (i + kb - 1) // kb, body, tuple(maxes))
    for g in range(n_heads // per):
        out_t = jnp.concatenate([acc_scr[hh, 0:hd, :] / acc_scr[hh, hd:hd + 1, :]
                                 for hh in range(g * per, (g + 1) * per)], axis=0)
        o_ref[0, :, g * LANES:(g + 1) * LANES] = out_t.T.astype(o_ref.dtype)


def _moba_prompt(q_t, k, v_t, kmean, hd):
    b, dm, t = q_t.shape
    blk = MOBA_BLOCK
    n_blocks = t // blk
    kb = 4 if n_blocks % 4 == 0 else 1
    width = 2 * LANES if dm % (2 * LANES) == 0 else LANES
    n_heads = width // hd
    bias = _moba_gate(q_t, kmean, hd, min(t, 1024))
    kern = functools.partial(_moba_prompt_kernel, hd=hd, blk=blk, kb=kb, scale=float(hd) ** -0.5)
    return pl.pallas_call(
        kern,
        out_shape=jax.ShapeDtypeStruct((b, t, dm), BF),
        grid=(b, dm // width, n_blocks),
        in_specs=[pl.BlockSpec((1, width, blk), lambda bb, hg, i: (bb, hg, i)),
                  pl.BlockSpec((1, t, width), lambda bb, hg, i: (bb, 0, hg)),
                  pl.BlockSpec((1, n_heads, v_t.shape[2], t), lambda bb, hg, i: (bb, hg, 0, 0)),
                  pl.BlockSpec((1, n_heads, n_blocks, blk), lambda bb, hg, i: (bb, hg, 0, i))],
        out_specs=pl.BlockSpec((1, blk, width), lambda bb, hg, i: (bb, i, hg)),
        scratch_shapes=[pltpu.VMEM((n_heads, LANES, blk), BF),
                        pltpu.VMEM((n_heads, v_t.shape[2], blk), F32)],
        compiler_params=_params("parallel", "parallel", "arbitrary"),
        name="moba_prompt",
    )(q_t, k, v_t, bias)


def _moba_decode_kernel(*refs, hd, ts, bps, n_new, n_blocks, topk, scale):
    n_pg = 2 * bps
    pt_ref, qm_ref, kn_ref, vn_ref = refs[:4]
    k_refs = refs[4:4 + n_pg]
    v_refs = refs[4 + n_pg:4 + 2 * n_pg]
    o_ref, qb_scr, km_scr, m_scr, l_scr, acc_all = refs[4 + 2 * n_pg:]
    j = pl.program_id(1)
    rows, dm = qm_ref.shape[1], qm_ref.shape[2]
    page = k_refs[0].shape[3]
    blk = 2 * page
    col_km = lax.broadcasted_iota(jnp.int32, km_scr.shape, 1)
    col_st = lax.broadcasted_iota(jnp.int32, m_scr.shape, 1)

    @pl.when(j == 0)
    def _():
        qb_scr[...] = (qm_ref[0] * (scale * LOG2_E)).astype(BF)
        km_scr[...] = jnp.zeros_like(km_scr)
        m_scr[...] = jnp.full(m_scr.shape, MASKED, F32)
        l_scr[...] = jnp.zeros_like(l_scr)

    qb = qb_scr[...]
    for b in range(bps):
        idx = j * bps + b
        ke = k_refs[2 * b][0].reshape(dm, page)
        ko = k_refs[2 * b + 1][0].reshape(dm, page)
        mean = jnp.sum(ke + ko, axis=-1, keepdims=True) * (1.0 / blk)
        km_scr[...] = jnp.where(col_km == idx, mean, km_scr[...])
        s = jnp.concatenate([_dot(qb, ke.astype(BF)), _dot(qb, ko.astype(BF))], axis=-1)
        cm = jnp.max(s, axis=-1, keepdims=True)
        p = jnp.exp2(s - cm)
        m_scr[...] = jnp.where(col_st == idx, cm, m_scr[...])
        l_scr[...] = jnp.where(col_st == idx, jnp.sum(p, axis=-1, keepdims=True), l_scr[...])
        pb = p.astype(BF)
        acc_all[idx] = (_dot_nt(pb[:, :page], v_refs[2 * b][0].reshape(dm, page).astype(BF))
                        + _dot_nt(pb[:, page:], v_refs[2 * b + 1][0].reshape(dm, page).astype(BF)))

    @pl.when(j == pl.num_programs(1) - 1)
    def _():
        gate = _gate_scores(qm_ref[0], km_scr[...])
        m_blk = m_scr[...] + _select_bias(gate, n_blocks, topk, 1)
        s = _dot_nt(qb, kn_ref[0].astype(BF))
        qpos = lax.broadcasted_iota(jnp.int32, s.shape, 0) & (ts - 1)
        kpos = lax.broadcasted_iota(jnp.int32, s.shape, 1)
        s = jnp.where((kpos <= qpos) & (kpos < n_new), s, MASKED)
        m_own = jnp.max(s, axis=-1, keepdims=True)
        p = jnp.exp2(s - m_own)
        m_all = jnp.maximum(m_own, jnp.max(m_blk, axis=-1, keepdims=True))
        w_own = jnp.exp2(m_own - m_all)
        w_blk = jnp.exp2(m_blk - m_all)
        l_all = (w_own * jnp.sum(p, axis=-1, keepdims=True)
                 + jnp.sum(w_blk * l_scr[...], axis=-1, keepdims=True))
        acc0 = w_own * _dot(p.astype(BF), vn_ref[0].astype(BF))

        def merge(jb, acc):
            w = jnp.sum(jnp.where(col_st == jb, w_blk, 0.0), axis=-1, keepdims=True)
            return acc + w * acc_all[jb]

        res = lax.fori_loop(0, n_blocks, merge, acc0) / l_all
        row_h = lax.broadcasted_iota(jnp.int32, (rows, dm), 0) >> (ts.bit_length() - 1)
        lane_h = lax.broadcasted_iota(jnp.int32, (rows, dm), 1) >> (hd.bit_length() - 1)
        own_head = jnp.where(row_h == lane_h, res, 0.0).astype(BF)
        pick = (lax.broadcasted_iota(jnp.int32, (o_ref.shape[1], rows), 1) & (ts - 1)
                == lax.broadcasted_iota(jnp.int32, (o_ref.shape[1], rows), 0)).astype(BF)
        o_ref[0] = _dot(pick, own_head).astype(o_ref.dtype)


def _moba_decode(page_table, qm, k_new, v_new, cache_k_t, cache_v_t, ts, qpad):
    s, rows, dm = qm.shape
    _, n_heads, hd, page = cache_k_t.shape
    n_blocks = page_table.shape[1] // 2
    bps = 4 if n_blocks % 4 == 0 else 1
    assert 2 * page == MOBA_BLOCK and n_blocks <= LANES and rows == n_heads * ts
    kern = functools.partial(_moba_decode_kernel, hd=hd, ts=ts, bps=bps, n_new=ts,
                             n_blocks=n_blocks, topk=MOBA_TOPK, scale=float(hd) ** -0.5)

    def page_spec(k):
        return pl.BlockSpec((1, n_heads, hd, page),
                            lambda ss, j, pt: (pt[ss, 2 * bps * j + k], 0, 0, 0))

    pages = [page_spec(k) for k in range(2 * bps)]
    per_seq = lambda a: pl.BlockSpec((1,) + a.shape[1:], lambda ss, j, pt: (ss, 0, 0))
    grid_spec = pltpu.PrefetchScalarGridSpec(
        num_scalar_prefetch=1,
        grid=(s, n_blocks // bps),
        in_specs=[per_seq(qm), per_seq(k_new), per_seq(v_new)] + pages + pages,
        out_specs=pl.BlockSpec((1, qpad, dm), lambda ss, j, pt: (ss, 0, 0)),
        scratch_shapes=[pltpu.VMEM((rows, dm), BF),
                        pltpu.VMEM((dm, LANES), F32),
                        pltpu.VMEM((rows, LANES), F32),
                        pltpu.VMEM((rows, LANES), F32),
                        pltpu.VMEM((n_blocks, rows, dm), F32)],
    )
    return pl.pallas_call(
        kern,
        out_shape=jax.ShapeDtypeStruct((s, qpad, dm), BF),
        grid_spec=grid_spec,
        compiler_params=_params("parallel", "arbitrary"),
        name="moba_decode",
    )(page_table, qm, k_new, v_new, *([cache_k_t] * (2 * bps)), *([cache_v_t] * (2 * bps)))


def _pad_rows(a, n, axis=1):
    pad = [(0, 0)] * a.ndim
    pad[axis] = (0, n - a.shape[axis])
    return jnp.pad(a, pad)


def _trunk(x, mods, kv_mod, gla_fns, conv_taps, seq_len, kvq_fn, attend, w, tiles):
    depth = len(mods)
    n_a = depth // 2
    alpha = float((2 * depth) ** 0.25)
    heads, dk, dv = w["gla_dims"]
    gla_states, conv_tails = [], []
    k_sh = v_sh = ctx = None
    for i in range(depth):
        sh_m, sc_m, g_m, sh_f, sc_f, g_f = mods[i]
        if i < n_a:
            q, k, v, g, gk = _gla_proj(x, sh_m, sc_m, w["a_w_in"][i], w["a_w_gk1"][i],
                                       w["a_w_gk2"][i], w["a_b_gk"][i], heads * dk, heads * dv,
                                       float(dk) ** -0.5, tiles["proj"])
            o, s_fin = gla_fns[i](q, k, v, g, gk)
            gla_states.append(s_fin)
            w_out = w["a_w_out"][i]
        else:
            o = attend(ctx)
            w_out = w["b_w_out"][i - n_a]
        x, tail = _ffn(x, o, g_m, w_out, w["ln_g"][i, 0:1], w["ln_b"][i, 0:1], sh_f, sc_f, g_f,
                       conv_taps[i], w["w_up"][i], w["w_conv"][i], w["b_conv"][i:i + 1],
                       w["w_down"][i], w["ln_g"][i, 1:2], w["ln_b"][i, 1:2], alpha, tiles["ffn"],
                       seq_len)
        conv_tails.append(tail)
        if i == n_a - 1:
            k_sh, v_sh, ctx = kvq_fn(x, kv_mod[0], kv_mod[1], mods[n_a][0], mods[n_a][1])
    return x, k_sh, v_sh, gla_states, conv_tails


def kernel(x_prompt, x_sample, cache_k, cache_v, state_gla, state_ffn_conv, page_table, c_prompt,
           c_sample, w_ada, b_ada, ln_g, ln_b, w_up, w_conv, b_conv, w_down, a_w_in, a_w_gk1,
           a_w_gk2, a_b_gk, a_norm_g, a_w_out, kv_w_ada, kv_b_ada, kv_w, b_w_q, b_w_out):
    bp, tp, d = x_prompt.shape
    bs, ts, _ = x_sample.shape
    depth = w_ada.shape[0]
    n_a = state_gla.shape[0]
    heads, dk, dv = state_gla.shape[2:]
    n_phys, page, m_heads, hd = cache_k.shape
    dm = m_heads * hd
    n_pages = page_table.shape[1]
    past_len = n_pages * page
    ff = w_down.shape[1]
    blk = MOBA_BLOCK
    assert depth == 2 * n_a == 2 and b_w_q.shape[0] == 1
    assert tp % blk == 0 and past_len % blk == 0
    assert 2 <= ts <= 8 and ts & (ts - 1) == 0
    assert LANES % hd == 0 and dm % LANES == 0

    w_k, w_v = kv_w[:, :dm].astype(BF), kv_w[:, dm:].astype(BF)
    w_q = b_w_q[0].astype(BF)
    w = {
        "gla_dims": (heads, dk, dv),
        "a_w_in": a_w_in.astype(BF),
        "a_w_gk1": _pad_rows(a_w_gk1, GLA_GATE_RANK_PAD, axis=2).astype(BF),
        "a_w_gk2": _pad_rows(a_w_gk2, GLA_GATE_RANK_PAD, axis=1).astype(BF),
        "a_b_gk": a_b_gk[:, None, :], "a_w_out": a_w_out.astype(BF),
        "w_up": w_up.astype(BF), "w_conv": w_conv, "b_conv": b_conv, "w_down": w_down.astype(BF),
        "ln_g": ln_g, "ln_b": ln_b, "b_w_out": b_w_out.astype(BF),
    }
    norm_g = [a_norm_g[i][None, :] for i in range(n_a)]

    n_c = bp + bs
    c_all = _pad_rows(jnp.concatenate([c_prompt, c_sample], axis=0), -(-n_c // 8) * 8, axis=0)
    mod_all = [_ada(c_all, w_ada, i, b_ada[i:i + 1]) for i in range(depth)]
    kv_mod_all = _ada(c_all, kv_w_ada[None], 0, kv_b_ada[None, :])

    def per_batch(m, n):
        return [m[:bp, j * d:(j + 1) * d][:, None, :] for j in range(n)]

    def per_token(m, n):
        return [jnp.repeat(m[bp:bp + bs, j * d:(j + 1) * d], ts, axis=0)[None] for j in range(n)]

    def gla_prompt(i):
        s0 = jnp.zeros((bp, heads, dv, dk), F32)
        return lambda q, k, v, g, gk: _gla_rec(q, k, v, g, gk, s0, norm_g[i], heads, dk, dv, 256)

    tables_p = _rope_tables(jnp.arange(tp, dtype=jnp.int32), hd)
    tiles_p = {"proj": 512, "row": 512, "ffn": 512}

    def kvq_prompt(x, *mods):
        k_t, v_t, k_row, v_tb, q_t, kmean = _kvq_prompt(x, mods, w_k, w_v, w_q, tables_p, hd,
                                                        tiles_p["row"])
        return k_t, v_t, (q_t, k_row, v_tb, kmean.reshape(bp, tp // blk, dm))

    zero_taps = [[jnp.zeros((bp, 2, ff), F32)] for _ in range(depth)]
    y_p, k_p, v_p, gla_p, conv_p = _trunk(
        x_prompt, [per_batch(m, 6) for m in mod_all], per_batch(kv_mod_all, 2),
        [gla_prompt(i) for i in range(n_a)], zero_taps, None, kvq_prompt,
        lambda ctx: _moba_prompt(*ctx, hd), w, tiles_p)

    n_rows = bs * ts
    xs = x_sample.reshape(1, n_rows, d)
    pos_s = past_len + jnp.arange(ts, dtype=jnp.int32)

    def gla_sample(i):
        s0 = jnp.swapaxes(state_gla[i], -1, -2)

        def run(q, k, v, g, gk):
            def seqs(a):
                return _pad_rows(a.reshape(bs, ts, a.shape[-1]), GLA_CHUNK)
            o, s_fin = _gla_rec(seqs(q), seqs(k), seqs(v), seqs(g), seqs(gk), s0, norm_g[i],
                                heads, dk, dv, GLA_CHUNK)
            return o[:, :ts].reshape(1, n_rows, heads * dv), s_fin
        return run

    qpad = 8
    cache_k_t = jnp.transpose(cache_k, (0, 2, 3, 1))
    cache_v_t = jnp.transpose(cache_v, (0, 2, 3, 1))
    tables_s = _rope_tables(jnp.tile(pos_s, bs), hd)
    tiles_s = {"proj": n_rows, "row": n_rows, "ffn": n_rows}

    def kvq_sample(x, *mods):
        k, v, q = _kvq_decode(x, mods, w_k, w_v, w_q, tables_s, hd, tiles_s["row"])
        return k, v, (q, k, v)

    def attend_sample(ctx):
        q, k, v = ctx
        seqs = lambda a, n: _pad_rows(a.reshape(bs, ts, dm), n)
        head_mask = (jnp.arange(dm)[None, :] // hd == jnp.arange(m_heads)[:, None]).astype(F32)
        qm = (q.reshape(bs, 1, ts, dm) * head_mask[None, :, None, :]).reshape(bs, m_heads * ts, dm)
        o = _moba_decode(page_table, qm, seqs(k, LANES), seqs(v, LANES), cache_k_t, cache_v_t,
                         ts, qpad)
        return o[:, :ts].reshape(1, n_rows, dm)

    def taps(i):
        st = state_ffn_conv[i]
        zeros = jnp.zeros((bs, ts - 1, ff), F32)
        tap1 = jnp.concatenate([st[:, 1:2], zeros], axis=1)
        tap2 = jnp.concatenate([st[:, 0:1], st[:, 1:2], zeros[:, :ts - 2]], axis=1)
        return [tap1.reshape(1, n_rows, ff), tap2.reshape(1, n_rows, ff)]

    y_s, k_s, v_s, gla_s, conv_s = _trunk(
        xs, [per_token(m, 6) for m in mod_all], per_token(kv_mod_all, 2),
        [gla_sample(i) for i in range(n_a)], [taps(i) for i in range(depth)], ts, kvq_sample,
        attend_sample, w, tiles_s)

    gla_state_p = jnp.stack([jnp.swapaxes(s, -1, -2) for s in gla_p])
    gla_state_s = jnp.stack([jnp.swapaxes(s, -1, -2) for s in gla_s])
    conv_state_p = jnp.stack(conv_p)
    conv_state_s = jnp.stack([g.reshape(bs, ts, ff)[:, ts - 2:] for g in conv_s])
    return (y_p, y_s.reshape(bs, ts, d),
            jnp.transpose(k_p, (0, 3, 1, 2)), jnp.transpose(v_p, (0, 3, 1, 2)),
            k_s.reshape(bs, ts, m_heads, hd), v_s.reshape(bs, ts, m_heads, hd),
            gla_state_p, gla_state_s, conv_state_p, conv_state_s)
```

```python
import functools

import jax
import jax.numpy as jnp
from jax import lax
from jax.experimental import pallas as pl
from jax.experimental.pallas import tpu as pltpu

F32 = jnp.float32
BF = jnp.bfloat16

LN_EPS = 1e-5
GLA_CHUNK = 64
GLA_GATE_NORM = 16.0
GLA_GATE_RANK_PAD = 128
MOBA_BLOCK = 256
MOBA_TOPK = 3
ROPE_THETA = 10000.0
MASKED = -1e30
LOG2_E = 1.4426950408889634
SUM_ROWS = 16
LANES = 128
VMEM_LIMIT = 56 * 1024 * 1024


def _params(*sem):
    return pltpu.CompilerParams(dimension_semantics=sem, vmem_limit_bytes=VMEM_LIMIT)


def _dot(a, b):
    return jnp.dot(a, b, preferred_element_type=F32)


def _dot_nt(a, b):
    return lax.dot_general(a, b, (((1,), (1,)), ((), ())), preferred_element_type=F32)


def _dot_tn(a, b):
    return lax.dot_general(a, b, (((0,), (0,)), ((), ())), preferred_element_type=F32)


def _layer_norm(z, g, b):
    mu = jnp.mean(z, axis=-1, keepdims=True)
    zc = z - mu
    var = jnp.mean(zc * zc, axis=-1, keepdims=True)
    return zc * lax.rsqrt(var + LN_EPS) * g + b


def _const_spec(shape):
    nd = len(shape)
    return pl.BlockSpec(shape, lambda *_: (0,) * nd, pipeline_mode=pl.Buffered(1))


def _mod_spec(arr, tm):
    if arr.shape[1] == 1:
        return pl.BlockSpec((1, 1, arr.shape[2]), lambda b, t: (b, 0, 0))
    return pl.BlockSpec((1, tm, arr.shape[2]), lambda b, t: (b, t, 0))


def _row_spec(tm, width):
    return pl.BlockSpec((1, tm, width), lambda b, t: (b, t, 0))


def _ada_kernel(c_ref, w_ref, b_ref, o_ref):
    c = c_ref[...]
    s = c * jax.nn.sigmoid(c)
    o_ref[...] = _dot(s.astype(BF), w_ref[...].astype(BF)) + b_ref[...]


def _ada(c, w3, layer, b2, tn=1024):
    r, d = c.shape
    n = w3.shape[2]
    return pl.pallas_call(
        _ada_kernel,
        out_shape=jax.ShapeDtypeStruct((r, n), F32),
        grid=(n // tn,),
        in_specs=[pl.BlockSpec((r, d), lambda j: (0, 0)),
                  pl.BlockSpec((None, d, tn), lambda j: (layer, 0, j)),
                  pl.BlockSpec((1, tn), lambda j: (0, j))],
        out_specs=pl.BlockSpec((r, tn), lambda j: (0, j)),
        compiler_params=_params("arbitrary"),
        name="ada_mod",
    )(c, w3, b2)


def _split3(x):
    hi = x.astype(BF)
    r = x - hi.astype(F32)
    mid = r.astype(BF)
    lo = (r - mid.astype(F32)).astype(BF)
    return hi, mid, lo


def _gla_kernel(x_ref, sh_ref, sc_ref, win_ref, wg1_ref, wg2_ref, bgk_ref, s0_ref, ng_ref,
                o_ref, sfin_ref, s_scr, q_scr, k_scr, v_scr, g_scr, gk_scr,
                *, heads, dk, dv, chunk, n_valid, scale):
    t = pl.program_id(1)
    tm = x_ref.shape[1]
    kd, vd = heads * dk, heads * dv

    @pl.when(t == 0)
    def _():
        s_scr[...] = s0_ref[0]

    h = (x_ref[0] * (1.0 + sc_ref[0]) + sh_ref[0]).astype(BF)
    q_scr[...] = _dot(h, win_ref[:, 0:kd]) * scale
    k = _dot(h, win_ref[:, kd:2 * kd])
    v_scr[...] = _dot(h, win_ref[:, 2 * kd:2 * kd + vd]).astype(BF)
    g_scr[...] = _dot(h, win_ref[:, 2 * kd + vd:2 * kd + 2 * vd])
    lin = _dot(_dot(h, wg1_ref[...]).astype(BF), wg2_ref[...]) + bgk_ref[...]
    gk = (jnp.minimum(lin, 0.0) - jnp.log1p(jnp.exp(-jnp.abs(lin)))) * (1.0 / GLA_GATE_NORM)
    if n_valid < tm:
        valid = lax.broadcasted_iota(jnp.int32, k.shape, 0) < n_valid
        k = jnp.where(valid, k, 0.0)
        gk = jnp.where(valid, gk, 0.0)
    k_scr[...] = k
    gk_scr[...] = gk

    row = lax.broadcasted_iota(jnp.int32, (chunk, chunk), 0)
    col = lax.broadcasted_iota(jnp.int32, (chunk, chunk), 1)
    tril_f = (row >= col).astype(F32)
    tril = tril_f.astype(BF)
    ng = ng_ref[...]

    for c in range(tm // chunk):
        rows = pl.ds(c * chunk, chunk)
        hi, mid, lo = _split3(gk_scr[rows, :])
        cum = _dot(tril, hi) + _dot(tril, mid) + _dot(tril, lo)
        last = cum[chunk - 1:chunk, :]
        q_in = (q_scr[rows, :] * jnp.exp(cum)).astype(BF)
        kc = k_scr[rows, :]
        k_in = (kc * jnp.exp(-cum)).astype(BF)
        k_end = (kc * jnp.exp(last - cum)).astype(BF)
        dec = jnp.exp(last)
        for hh in range(heads):
            ks = slice(hh * dk, (hh + 1) * dk)
            vs = slice(hh * dv, (hh + 1) * dv)
            a = _dot_nt(q_in[:, ks], k_in[:, ks]) * tril_f
            vh = v_scr[rows, vs]
            st = s_scr[hh]
            o = _dot(a.astype(BF), vh) + _dot_nt(q_in[:, ks], st.astype(BF))
            s_scr[hh] = st * dec[:, ks] + _dot_tn(vh, k_end[:, ks])
            o = o * lax.rsqrt(jnp.mean(o * o, axis=-1, keepdims=True) + LN_EPS) * ng
            gh = g_scr[rows, vs]
            o_ref[0, rows, vs] = (o * (gh * jax.nn.sigmoid(gh))).astype(BF)

    @pl.when(t == pl.num_programs(1) - 1)
    def _():
        sfin_ref[0] = s_scr[...]


def _gla(x, sh, sc, w_in, wg1, wg2, bgk, s0t, norm_g, heads, dk, dv, tm, n_valid):
    b, t, d = x.shape
    kd, vd = heads * dk, heads * dv
    kern = functools.partial(_gla_kernel, heads=heads, dk=dk, dv=dv, chunk=GLA_CHUNK,
                             n_valid=n_valid, scale=float(dk) ** -0.5)
    st_spec = pl.BlockSpec((1, heads, dv, dk), lambda bb, tt: (bb, 0, 0, 0))
    return pl.pallas_call(
        kern,
        out_shape=[jax.ShapeDtypeStruct((b, t, vd), BF),
                   jax.ShapeDtypeStruct((b, heads, dv, dk), F32)],
        grid=(b, t // tm),
        in_specs=[_row_spec(tm, d), _mod_spec(sh, tm), _mod_spec(sc, tm),
                  _const_spec(w_in.shape), _const_spec(wg1.shape), _const_spec(wg2.shape),
                  _const_spec(bgk.shape), st_spec, _const_spec(norm_g.shape)],
        out_specs=[_row_spec(tm, vd), st_spec],
        scratch_shapes=[pltpu.VMEM((heads, dv, dk), F32), pltpu.VMEM((tm, kd), F32),
                        pltpu.VMEM((tm, kd), F32), pltpu.VMEM((tm, vd), BF),
                        pltpu.VMEM((tm, vd), F32), pltpu.VMEM((tm, kd), F32)],
        compiler_params=_params("parallel", "arbitrary"),
        name="gla",
    )(x, sh, sc, w_in, wg1, wg2, bgk, s0t, norm_g)


def _ffn_kernel(*refs, alpha, ff, fc, tm, seq_len):
    x_ref, o_ref, gm_ref, wout_ref, lng0_ref, lnb0_ref, sh_ref, sc_ref, gate_ref = refs[:9]
    if seq_len is None:
        (st_ref, wup_ref, wconv_ref, bconv_ref, wdown_ref,
         lng_ref, lnb_ref, out_ref, tail_ref, act, carry) = refs[9:]
    else:
        (tap1_ref, tap2_ref, wup_ref, wconv_ref, bconv_ref,
         wdown_ref, lng_ref, lnb_ref, out_ref, tail_ref, act) = refs[9:]
    t = pl.program_id(1)
    x = _layer_norm(alpha * x_ref[0] + (1.0 + gm_ref[0]) * _dot(o_ref[0], wout_ref[...]),
                    lng0_ref[...], lnb0_ref[...])
    h = (x * (1.0 + sc_ref[0]) + sh_ref[0]).astype(BF)
    rowi = lax.broadcasted_iota(jnp.int32, (tm, fc), 0)
    for c in range(ff // fc):
        cs = slice(c * fc, (c + 1) * fc)
        u = _dot(h, wup_ref[:, c * fc:(c + 1) * fc])
        g = _dot(h, wup_ref[:, ff + c * fc:ff + (c + 1) * fc])
        g1 = pltpu.roll(g, 1, 0)
        g2 = pltpu.roll(g, 2, 0)
        if seq_len is None:
            prev = jnp.where(t == 0, st_ref[0, :, cs], carry[0:2, cs])
            g1 = jnp.where(rowi == 0, prev[1:2, :], g1)
            g2 = jnp.where(rowi == 0, prev[0:1, :], jnp.where(rowi == 1, prev[1:2, :], g2))
            carry[0:2, cs] = g[tm - 2:tm, :]
            tail_ref[0, :, cs] = g[tm - 2:tm, :]
        else:
            pos = rowi & (seq_len - 1)
            g1 = jnp.where(pos >= 1, g1, tap1_ref[0, :, cs])
            g2 = jnp.where(pos >= 2, g2, tap2_ref[0, :, cs])
            tail_ref[0, :, cs] = g
        gc = (bconv_ref[:, cs] + g2 * wconv_ref[0:1, cs] + g1 * wconv_ref[1:2, cs]
              + g * wconv_ref[2:3, cs])
        act[:, cs] = (0.5 * gc * (1.0 + lax.erf(gc * 0.7071067811865476)) * u).astype(BF)
    z = alpha * x + (1.0 + gate_ref[0]) * _dot(act[...], wdown_ref[...])
    out_ref[0] = _layer_norm(z, lng_ref[...], lnb_ref[...])


def _ffn(x, o, gate_m, w_out, lng0, lnb0, sh, sc, gate, taps, w_up, w_conv, b_conv, w_down, lng,
         lnb, alpha, tm, seq_len):
    b, t, d = x.shape
    ff = w_down.shape[0]
    fc = 256 if ff % 256 == 0 else LANES
    kern = functools.partial(_ffn_kernel, alpha=alpha, ff=ff, fc=fc, tm=tm, seq_len=seq_len)
    wspecs = [_const_spec(w_up.shape), _const_spec(w_conv.shape), _const_spec(b_conv.shape),
              _const_spec(w_down.shape), _const_spec(lng.shape), _const_spec(lnb.shape)]
    mixer_specs = [_row_spec(tm, d), _row_spec(tm, o.shape[2]), _mod_spec(gate_m, tm),
                   _const_spec(w_out.shape), _const_spec(lng0.shape), _const_spec(lnb0.shape)]
    if seq_len is None:
        tap_specs = [pl.BlockSpec((1, 2, ff), lambda bb, tt: (bb, 0, 0))]
        tail_shape = jax.ShapeDtypeStruct((b, 2, ff), F32)
        tail_spec = pl.BlockSpec((1, 2, ff), lambda bb, tt: (bb, 0, 0))
        scratch = [pltpu.VMEM((tm, ff), BF), pltpu.VMEM((8, ff), F32)]
        sem = ("parallel", "arbitrary")
    else:
        assert seq_len & (seq_len - 1) == 0 and tm % seq_len == 0
        tap_specs = [_row_spec(tm, ff), _row_spec(tm, ff)]
        tail_shape = jax.ShapeDtypeStruct((b, t, ff), F32)
        tail_spec = _row_spec(tm, ff)
        scratch = [pltpu.VMEM((tm, ff), BF)]
        sem = ("parallel", "parallel")
    return pl.pallas_call(
        kern,
        out_shape=[jax.ShapeDtypeStruct((b, t, d), F32), tail_shape],
        grid=(b, t // tm),
        in_specs=mixer_specs + [_mod_spec(sh, tm), _mod_spec(sc, tm), _mod_spec(gate, tm)]
        + tap_specs + wspecs,
        out_specs=[_row_spec(tm, d), tail_spec],
        scratch_shapes=scratch,
        compiler_params=_params(*sem),
        name="conv_ffn",
    )(x, o, gate_m, w_out, lng0, lnb0, sh, sc, gate, *taps, w_up, w_conv, b_conv, w_down, lng, lnb)


def _rope_rows(x, cos, sin, hd):
    half = hd // 2
    lane = lax.broadcasted_iota(jnp.int32, (x.shape[0], LANES), 1)
    first = (lane & (hd - 1)) < half
    out = []
    for s in range(x.shape[1] // LANES):
        xs = x[:, s * LANES:(s + 1) * LANES]
        partner = jnp.where(first, pltpu.roll(xs, LANES - half, 1), pltpu.roll(xs, half, 1))
        out.append(xs * cos + partner * sin)
    return out


def _rope_cols(x3, cos_t, sin_t):
    half = x3.shape[1] // 2
    partner = jnp.concatenate([x3[:, half:, :], x3[:, :half, :]], axis=1)
    return x3 * cos_t[None] + partner * sin_t[None]


def _kvq_decode_kernel(x_ref, shk_ref, sck_ref, shq_ref, scq_ref, wk_ref, wv_ref, wq_ref,
                       cos_ref, sin_ref, k_ref, v_ref, q_ref, *, hd):
    x = x_ref[0]
    hk = (x * (1.0 + sck_ref[0]) + shk_ref[0]).astype(BF)
    hq = (x * (1.0 + scq_ref[0]) + shq_ref[0]).astype(BF)
    cos = cos_ref[...]
    sin = sin_ref[...]
    for s, piece in enumerate(_rope_rows(_dot(hk, wk_ref[...]), cos, sin, hd)):
        k_ref[0, :, s * LANES:(s + 1) * LANES] = piece
    v_ref[0] = _dot(hk, wv_ref[...])
    for s, piece in enumerate(_rope_rows(_dot(hq, wq_ref[...]), cos, sin, hd)):
        q_ref[0, :, s * LANES:(s + 1) * LANES] = piece


def _kvq_prompt_kernel(x_ref, shk_ref, sck_ref, shq_ref, scq_ref, wk_ref, wkt_ref, wvt_ref,
                       wqt_ref, cos_ref, sin_ref, cos_t_ref, sin_t_ref,
                       kt_ref, vt_ref, krow_ref, vtb_ref, qt_ref, km_ref, *, hd, blk):
    x = x_ref[0]
    tm = x.shape[0]
    hk = (x * (1.0 + sck_ref[0]) + shk_ref[0]).astype(BF)
    hq = (x * (1.0 + scq_ref[0]) + shq_ref[0]).astype(BF)
    heads = kt_ref.shape[1]
    cos_t = cos_t_ref[...]
    sin_t = sin_t_ref[...]
    kt_ref[0] = _rope_cols(_dot_nt(wkt_ref[...], hk).reshape(heads, hd, tm), cos_t, sin_t)
    vt = _dot_nt(wvt_ref[...], hk).reshape(heads, hd, tm)
    vt_ref[0] = vt
    vtb_ref[0, :, 0:hd, :] = vt.astype(BF)
    vtb_ref[0, :, hd:, :] = jnp.ones((heads, vtb_ref.shape[2] - hd, tm), BF)
    qt_ref[0] = _rope_cols(_dot_nt(wqt_ref[...], hq).reshape(heads, hd, tm), cos_t,
                           sin_t).reshape(heads * hd, tm)
    for s, piece in enumerate(_rope_rows(_dot(hk, wk_ref[...]), cos_ref[...], sin_ref[...], hd)):
        cols = slice(s * LANES, (s + 1) * LANES)
        krow_ref[0, :, cols] = piece.astype(BF)
        for j in range(tm // blk):
            km_ref[0, 0, j:j + 1, cols] = jnp.mean(piece[j * blk:(j + 1) * blk, :], axis=0,
                                                   keepdims=True)


def _kvq_specs(x, mods, tm):
    return [_row_spec(tm, x.shape[2])] + [_mod_spec(m, tm) for m in mods]


def _kvq_decode(x, mods, wk, wv, wq, tables, hd, tm):
    b, t, d = x.shape
    dm = wq.shape[1]
    cos, sin = tables[:2]
    tab_spec = pl.BlockSpec((tm, LANES), lambda bb, tt: (tt, 0))
    row = jax.ShapeDtypeStruct((b, t, dm), F32)
    return pl.pallas_call(
        functools.partial(_kvq_decode_kernel, hd=hd),
        out_shape=[row, row, row],
        grid=(b, t // tm),
        in_specs=_kvq_specs(x, mods, tm) + [_const_spec(wk.shape), _const_spec(wv.shape),
                                            _const_spec(wq.shape), tab_spec, tab_spec],
        out_specs=[_row_spec(tm, dm)] * 3,
        compiler_params=_params("parallel", "parallel"),
        name="kvq_decode",
    )(x, *mods, wk, wv, wq, cos, sin)


def _kvq_prompt(x, mods, wk, wv, wq, tables, hd, tm):
    b, t, d = x.shape
    dm = wq.shape[1]
    heads = dm // hd
    blk = MOBA_BLOCK
    cos, sin, cos_t, sin_t = tables
    tab_spec = pl.BlockSpec((tm, LANES), lambda bb, tt: (tt, 0))
    tab_t_spec = pl.BlockSpec((hd, tm), lambda bb, tt: (0, tt))
    col_spec = pl.BlockSpec((1, heads, hd, tm), lambda bb, tt: (bb, 0, 0, tt))
    w_spec = _const_spec(wk.shape)
    return pl.pallas_call(
        functools.partial(_kvq_prompt_kernel, hd=hd, blk=blk),
        out_shape=[jax.ShapeDtypeStruct((b, heads, hd, t), F32),
                   jax.ShapeDtypeStruct((b, heads, hd, t), F32),
                   jax.ShapeDtypeStruct((b, t, dm), BF),
                   jax.ShapeDtypeStruct((b, heads, hd + SUM_ROWS, t), BF),
                   jax.ShapeDtypeStruct((b, dm, t), F32),
                   jax.ShapeDtypeStruct((b, t // tm, tm // blk, dm), F32)],
        grid=(b, t // tm),
        in_specs=_kvq_specs(x, mods, tm) + [w_spec, w_spec, w_spec, w_spec, tab_spec, tab_spec,
                                            tab_t_spec, tab_t_spec],
        out_specs=[col_spec, col_spec, _row_spec(tm, dm),
                   pl.BlockSpec((1, heads, hd + SUM_ROWS, tm), lambda bb, tt: (bb, 0, 0, tt)),
                   pl.BlockSpec((1, dm, tm), lambda bb, tt: (bb, 0, tt)),
                   pl.BlockSpec((1, 1, tm // blk, dm), lambda bb, tt: (bb, tt, 0, 0))],
        compiler_params=_params("parallel", "parallel"),
        name="kvq_prompt",
    )(x, *mods, wk, wk.T, wv.T, wq.T, cos, sin, cos_t, sin_t)


def _rope_tables(pos, hd):
    half = hd // 2
    inv = ROPE_THETA ** (-jnp.arange(half, dtype=F32) / half)
    ang = pos.astype(F32)[:, None] * inv[None, :]
    cos, sin = jnp.cos(ang), jnp.sin(ang)
    cos_h = jnp.concatenate([cos, cos], axis=1)
    sin_h = jnp.concatenate([-sin, sin], axis=1)
    reps = LANES // hd
    return jnp.tile(cos_h, (1, reps)), jnp.tile(sin_h, (1, reps)), cos_h.T, sin_h.T


def _select_bias(gate, n_valid, topk, axis):
    bidx = lax.broadcasted_iota(jnp.int32, gate.shape, axis)
    bidx_f = bidx.astype(F32)
    ninf = jnp.float32(-jnp.inf)
    gate = jnp.where(bidx < n_valid, gate, ninf)
    sel = jnp.zeros(gate.shape, jnp.bool_)
    for _ in range(topk):
        mx = jnp.max(gate, axis=axis, keepdims=True)
        first = jnp.min(jnp.where(gate == mx, bidx_f, jnp.float32(gate.shape[axis])), axis=axis,
                        keepdims=True)
        pick = (bidx_f == first) & (mx > ninf)
        sel = sel | pick
        gate = jnp.where(pick, ninf, gate)
    return jnp.where(sel, 0.0, MASKED)


def _gate_scores(q, km):
    return jnp.dot(q, km, preferred_element_type=F32, precision=lax.Precision.HIGHEST)


def _head_rows(qt, hh, hd):
    rowi = lax.broadcasted_iota(jnp.int32, qt.shape, 0)
    return jnp.where((rowi >= hh * hd) & (rowi < (hh + 1) * hd), qt, 0.0)


def _moba_gate_kernel(qt_ref, km_ref, bias_ref, *, hd, blk, topk):
    tq = qt_ref.shape[2]
    km = km_ref[0]
    nb = km.shape[0]
    pos = pl.program_id(2) * tq + lax.broadcasted_iota(jnp.int32, (1, tq), 1)
    own = pos >> (blk.bit_length() - 1)
    lane = lax.broadcasted_iota(jnp.int32, km.shape, 1)
    per = LANES // hd
    km_heads = jnp.concatenate(
        [jnp.where((lane >= hh * hd) & (lane < (hh + 1) * hd), km, 0.0) for hh in range(per)],
        axis=0)
    gate = _gate_scores(km_heads, qt_ref[0])
    for hh in range(per):
        bias_ref[0, hh] = _select_bias(gate[hh * nb:(hh + 1) * nb, :], own, topk, 0)


def _moba_gate(q_t, kmean, hd, tq):
    b, dm, t = q_t.shape
    n_blocks = kmean.shape[1]
    n_heads = LANES // hd
    return pl.pallas_call(
        functools.partial(_moba_gate_kernel, hd=hd, blk=MOBA_BLOCK, topk=MOBA_TOPK),
        out_shape=jax.ShapeDtypeStruct((b, dm // hd, n_blocks, t), F32),
        grid=(b, dm // LANES, t // tq),
        in_specs=[pl.BlockSpec((1, LANES, tq), lambda bb, hp, tt: (bb, hp, tt)),
                  pl.BlockSpec((1, n_blocks, LANES), lambda bb, hp, tt: (bb, 0, hp))],
        out_specs=pl.BlockSpec((1, n_heads, n_blocks, tq), lambda bb, hp, tt: (bb, hp, 0, tt)),
        compiler_params=_params("parallel", "parallel", "parallel"),
        name="moba_gate",
    )(q_t, kmean)


def _moba_prompt_kernel(qt_ref, k_ref, vt_ref, bias_ref, o_ref, qb_scr, acc_scr,
                        *, hd, blk, kb, scale):
    i = pl.program_id(2)
    per = LANES // hd
    n_heads = vt_ref.shape[1]
    heads = range(n_heads)
    key = lax.broadcasted_iota(jnp.int32, (blk, blk), 0)
    qry = lax.broadcasted_iota(jnp.int32, (blk, blk), 1)
    own = pl.multiple_of(i * blk, blk)

    def k_tile(hh, start):
        g = hh // per
        return k_ref[0, pl.ds(start, blk), g * LANES:(g + 1) * LANES]

    for hh in heads:
        g = hh // per
        qh = _head_rows(qt_ref[0, g * LANES:(g + 1) * LANES, :], hh % per, hd)
        qb_scr[hh] = (qh * (scale * LOG2_E)).astype(BF)
    own_s = [jnp.where(key <= qry, _dot(k_tile(hh, own), qb_scr[hh]), MASKED) for hh in heads]
    maxes = [jnp.max(s, axis=0, keepdims=True) for s in own_s]
    own_p = [jnp.exp2(s - m).astype(BF) for s, m in zip(own_s, maxes)]
    for hh in heads:
        acc_scr[hh] = _dot(vt_ref[0, hh, :, pl.ds(own, blk)], own_p[hh])

    def body(it, carry):
        tiles = [(hh, b, pl.multiple_of((it * kb + b) * blk, blk))
                 for b in range(kb) for hh in heads]
        scores = [_dot(k_tile(hh, start), qb_scr[hh]) for hh, b, start in tiles]
        parts = [[] for _ in heads]
        probs = []
        for (hh, b, start), s in zip(tiles, scores):
            cm = jnp.max(s, axis=0, keepdims=True)
            probs.append(jnp.exp2(s - cm).astype(BF))
            parts[hh].append([cm + bias_ref[0, hh, pl.ds(it * kb + b, 1), :]])
        for (hh, b, start), p in zip(tiles, probs):
            parts[hh][b].append(_dot(vt_ref[0, hh, :, pl.ds(start, blk)], p))
        new = []
        for hh in heads:
            m_prev = carry[hh]
            m_new = m_prev
            for m_b, _ in parts[hh]:
                m_new = jnp.maximum(m_new, m_b)
            acc = jnp.exp2(m_prev - m_new) * acc_scr[hh]
            for m_b, acc_b in parts[hh]:
                acc = acc + jnp.exp2(m_b - m_new) * acc_b
            acc_scr[hh] = acc
            new.append(m_new)
        return tuple(new)

    lax.fori_loop(0, (i + kb - 1) // kb, body, tuple(maxes))
    for g in range(n_heads // per):
        out_t = jnp.concatenate([acc_scr[hh, 0:hd, :] / acc_scr[hh, hd:hd + 1, :]
                                 for hh in range(g * per, (g + 1) * per)], axis=0)
        o_ref[0, :, g * LANES:(g + 1) * LANES] = out_t.T.astype(o_ref.dtype)


def _moba_prompt(q_t, k, v_t, kmean, hd):
    b, dm, t = q_t.shape
    blk = MOBA_BLOCK
    n_blocks = t // blk
    kb = 2 if n_blocks % 2 == 0 else 1
    width = 4 * LANES if dm % (4 * LANES) == 0 else LANES
    n_heads = width // hd
    bias = _moba_gate(q_t, kmean, hd, min(t, 1024))
    kern = functools.partial(_moba_prompt_kernel, hd=hd, blk=blk, kb=kb, scale=float(hd) ** -0.5)
    return pl.pallas_call(
        kern,
        out_shape=jax.ShapeDtypeStruct((b, t, dm), BF),
        grid=(b, dm // width, n_blocks),
        in_specs=[pl.BlockSpec((1, width, blk), lambda bb, hg, i: (bb, hg, i)),
                  pl.BlockSpec((1, t, width), lambda bb, hg, i: (bb, 0, hg)),
                  pl.BlockSpec((1, n_heads, v_t.shape[2], t), lambda bb, hg, i: (bb, hg, 0, 0)),
                  pl.BlockSpec((1, n_heads, n_blocks, blk), lambda bb, hg, i: (bb, hg, 0, i))],
        out_specs=pl.BlockSpec((1, blk, width), lambda bb, hg, i: (bb, i, hg)),
        scratch_shapes=[pltpu.VMEM((n_heads, LANES, blk), BF),
                        pltpu.VMEM((n_heads, v_t.shape[2], blk), F32)],
        compiler_params=_params("parallel", "parallel", "arbitrary"),
        name="moba_prompt",
    )(q_t, k, v_t, bias)


def _moba_decode_kernel(*refs, hd, ts, bps, n_new, n_blocks, topk, scale):
    n_pg = 2 * bps
    pt_ref, qm_ref, kn_ref, vn_ref = refs[:4]
    k_refs = refs[4:4 + n_pg]
    v_refs = refs[4 + n_pg:4 + 2 * n_pg]
    o_ref, qb_scr, km_scr, m_scr, l_scr, acc_all = refs[4 + 2 * n_pg:]
    j = pl.program_id(1)
    rows, dm = qm_ref.shape[1], qm_ref.shape[2]
    page = k_refs[0].shape[3]
    blk = 2 * page
    col_km = lax.broadcasted_iota(jnp.int32, km_scr.shape, 1)
    col_st = lax.broadcasted_iota(jnp.int32, m_scr.shape, 1)

    @pl.when(j == 0)
    def _():
        qb_scr[...] = (qm_ref[0] * (scale * LOG2_E)).astype(BF)
        km_scr[...] = jnp.zeros_like(km_scr)
        m_scr[...] = jnp.full(m_scr.shape, MASKED, F32)
        l_scr[...] = jnp.zeros_like(l_scr)

    qb = qb_scr[...]
    for b in range(bps):
        idx = j * bps + b
        ke = k_refs[2 * b][0].reshape(dm, page)
        ko = k_refs[2 * b + 1][0].reshape(dm, page)
        mean = jnp.sum(ke + ko, axis=-1, keepdims=True) * (1.0 / blk)
        km_scr[...] = jnp.where(col_km == idx, mean, km_scr[...])
        s = jnp.concatenate([_dot(qb, ke.astype(BF)), _dot(qb, ko.astype(BF))], axis=-1)
        cm = jnp.max(s, axis=-1, keepdims=True)
        p = jnp.exp2(s - cm)
        m_scr[...] = jnp.where(col_st == idx, cm, m_scr[...])
        l_scr[...] = jnp.where(col_st == idx, jnp.sum(p, axis=-1, keepdims=True), l_scr[...])
        pb = p.astype(BF)
        acc_all[idx] = (_dot_nt(pb[:, :page], v_refs[2 * b][0].reshape(dm, page).astype(BF))
                        + _dot_nt(pb[:, page:], v_refs[2 * b + 1][0].reshape(dm, page).astype(BF)))

    @pl.when(j == pl.num_programs(1) - 1)
    def _():
        gate = _gate_scores(qm_ref[0], km_scr[...])
        m_blk = m_scr[...] + _select_bias(gate, n_blocks, topk, 1)
        s = _dot_nt(qb, kn_ref[0].astype(BF))
        qpos = lax.broadcasted_iota(jnp.int32, s.shape, 0) & (ts - 1)
        kpos = lax.broadcasted_iota(jnp.int32, s.shape, 1)
        s = jnp.where((kpos <= qpos) & (kpos < n_new), s, MASKED)
        m_own = jnp.max(s, axis=-1, keepdims=True)
        p = jnp.exp2(s - m_own)
        m_all = jnp.maximum(m_own, jnp.max(m_blk, axis=-1, keepdims=True))
        w_own = jnp.exp2(m_own - m_all)
        w_blk = jnp.exp2(m_blk - m_all)
        l_all = (w_own * jnp.sum(p, axis=-1, keepdims=True)
                 + jnp.sum(w_blk * l_scr[...], axis=-1, keepdims=True))
        acc0 = w_own * _dot(p.astype(BF), vn_ref[0].astype(BF))

        def merge(jb, acc):
            w = jnp.sum(jnp.where(col_st == jb, w_blk, 0.0), axis=-1, keepdims=True)
            return acc + w * acc_all[jb]

        res = lax.fori_loop(0, n_blocks, merge, acc0) / l_all
        row_h = lax.broadcasted_iota(jnp.int32, (rows, dm), 0) >> (ts.bit_length() - 1)
        lane_h = lax.broadcasted_iota(jnp.int32, (rows, dm), 1) >> (hd.bit_length() - 1)
        own_head = jnp.where(row_h == lane_h, res, 0.0).astype(BF)
        pick = (lax.broadcasted_iota(jnp.int32, (o_ref.shape[1], rows), 1) & (ts - 1)
                == lax.broadcasted_iota(jnp.int32, (o_ref.shape[1], rows), 0)).astype(BF)
        o_ref[0] = _dot(pick, own_head).astype(o_ref.dtype)


def _moba_decode(page_table, qm, k_new, v_new, cache_k_t, cache_v_t, ts, qpad):
    s, rows, dm = qm.shape
    _, n_heads, hd, page = cache_k_t.shape
    n_blocks = page_table.shape[1] // 2
    bps = 4 if n_blocks % 4 == 0 else 1
    assert 2 * page == MOBA_BLOCK and n_blocks <= LANES and rows == n_heads * ts
    kern = functools.partial(_moba_decode_kernel, hd=hd, ts=ts, bps=bps, n_new=ts,
                             n_blocks=n_blocks, topk=MOBA_TOPK, scale=float(hd) ** -0.5)

    def page_spec(k):
        return pl.BlockSpec((1, n_heads, hd, page),
                            lambda ss, j, pt: (pt[ss, 2 * bps * j + k], 0, 0, 0))

    pages = [page_spec(k) for k in range(2 * bps)]
    per_seq = lambda a: pl.BlockSpec((1,) + a.shape[1:], lambda ss, j, pt: (ss, 0, 0))
    grid_spec = pltpu.PrefetchScalarGridSpec(
        num_scalar_prefetch=1,
        grid=(s, n_blocks // bps),
        in_specs=[per_seq(qm), per_seq(k_new), per_seq(v_new)] + pages + pages,
        out_specs=pl.BlockSpec((1, qpad, dm), lambda ss, j, pt: (ss, 0, 0)),
        scratch_shapes=[pltpu.VMEM((rows, dm), BF),
                        pltpu.VMEM((dm, LANES), F32),
                        pltpu.VMEM((rows, LANES), F32),
                        pltpu.VMEM((rows, LANES), F32),
                        pltpu.VMEM((n_blocks, rows, dm), F32)],
    )
    return pl.pallas_call(
        kern,
        out_shape=jax.ShapeDtypeStruct((s, qpad, dm), BF),
        grid_spec=grid_spec,
        compiler_params=_params("parallel", "arbitrary"),
        name="moba_decode",
    )(page_table, qm, k_new, v_new, *([cache_k_t] * (2 * bps)), *([cache_v_t] * (2 * bps)))


def _pad_rows(a, n, axis=1):
    pad = [(0, 0)] * a.ndim
    pad[axis] = (0, n - a.shape[axis])
    return jnp.pad(a, pad)


def _trunk(x, mods, kv_mod, gla_fns, conv_taps, seq_len, kvq_fn, attend, w, tiles):
    depth = len(mods)
    n_a = depth // 2
    alpha = float((2 * depth) ** 0.25)
    gla_states, conv_tails = [], []
    k_sh = v_sh = ctx = None
    for i in range(depth):
        sh_m, sc_m, g_m, sh_f, sc_f, g_f = mods[i]
        if i < n_a:
            o, s_fin = gla_fns[i](x)
            gla_states.append(s_fin)
            w_out = w["a_w_out"][i]
        else:
            o = attend(ctx)
            w_out = w["b_w_out"][i - n_a]
        x, tail = _ffn(x, o, g_m, w_out, w["ln_g"][i, 0:1], w["ln_b"][i, 0:1], sh_f, sc_f, g_f,
                       conv_taps[i], w["w_up"][i], w["w_conv"][i], w["b_conv"][i:i + 1],
                       w["w_down"][i], w["ln_g"][i, 1:2], w["ln_b"][i, 1:2], alpha, tiles["ffn"],
                       seq_len)
        conv_tails.append(tail)
        if i == n_a - 1:
            k_sh, v_sh, ctx = kvq_fn(x, kv_mod[0], kv_mod[1], mods[n_a][0], mods[n_a][1])
    return x, k_sh, v_sh, gla_states, conv_tails


def kernel(x_prompt, x_sample, cache_k, cache_v, state_gla, state_ffn_conv, page_table, c_prompt,
           c_sample, w_ada, b_ada, ln_g, ln_b, w_up, w_conv, b_conv, w_down, a_w_in, a_w_gk1,
           a_w_gk2, a_b_gk, a_norm_g, a_w_out, kv_w_ada, kv_b_ada, kv_w, b_w_q, b_w_out):
    bp, tp, d = x_prompt.shape
    bs, ts, _ = x_sample.shape
    depth = w_ada.shape[0]
    n_a = state_gla.shape[0]
    heads, dk, dv = state_gla.shape[2:]
    n_phys, page, m_heads, hd = cache_k.shape
    dm = m_heads * hd
    n_pages = page_table.shape[1]
    past_len = n_pages * page
    ff = w_down.shape[1]
    blk = MOBA_BLOCK
    assert depth == 2 * n_a == 2 and b_w_q.shape[0] == 1
    assert tp % blk == 0 and past_len % blk == 0
    assert 2 <= ts <= 8 and ts & (ts - 1) == 0
    assert LANES % hd == 0 and dm % LANES == 0

    w_k, w_v = kv_w[:, :dm].astype(BF), kv_w[:, dm:].astype(BF)
    w_q = b_w_q[0].astype(BF)
    w = {
        "a_w_in": a_w_in.astype(BF),
        "a_w_gk1": _pad_rows(a_w_gk1, GLA_GATE_RANK_PAD, axis=2).astype(BF),
        "a_w_gk2": _pad_rows(a_w_gk2, GLA_GATE_RANK_PAD, axis=1).astype(BF),
        "a_b_gk": a_b_gk[:, None, :], "a_w_out": a_w_out.astype(BF),
        "w_up": w_up.astype(BF), "w_conv": w_conv, "b_conv": b_conv, "w_down": w_down.astype(BF),
        "ln_g": ln_g, "ln_b": ln_b, "b_w_out": b_w_out.astype(BF),
    }
    norm_g = [a_norm_g[i][None, :] for i in range(n_a)]

    n_c = bp + bs
    c_all = _pad_rows(jnp.concatenate([c_prompt, c_sample], axis=0), -(-n_c // 8) * 8, axis=0)
    mod_all = [_ada(c_all, w_ada, i, b_ada[i:i + 1]) for i in range(depth)]
    kv_mod_all = _ada(c_all, kv_w_ada[None], 0, kv_b_ada[None, :])

    def per_batch(m, n):
        return [m[:bp, j * d:(j + 1) * d][:, None, :] for j in range(n)]

    def per_token(m, n):
        return [jnp.repeat(m[bp:bp + bs, j * d:(j + 1) * d], ts, axis=0)[None] for j in range(n)]

    def gla_weights(i):
        return (w["a_w_in"][i], w["a_w_gk1"][i], w["a_w_gk2"][i], w["a_b_gk"][i])

    mods_p = [per_batch(m, 6) for m in mod_all]

    def gla_prompt(i):
        s0 = jnp.zeros((bp, heads, dv, dk), F32)
        return lambda x: _gla(x, mods_p[i][0], mods_p[i][1], *gla_weights(i), s0, norm_g[i],
                              heads, dk, dv, tiles_p["gla"], tiles_p["gla"])

    tables_p = _rope_tables(jnp.arange(tp, dtype=jnp.int32), hd)
    tiles_p = {"gla": 512, "row": 512, "ffn": 512}

    def kvq_prompt(x, *mods):
        k_t, v_t, k_row, v_tb, q_t, kmean = _kvq_prompt(x, mods, w_k, w_v, w_q, tables_p, hd,
                                                        tiles_p["row"])
        return k_t, v_t, (q_t, k_row, v_tb, kmean.reshape(bp, tp // blk, dm))

    zero_taps = [[jnp.zeros((bp, 2, ff), F32)] for _ in range(depth)]
    y_p, k_p, v_p, gla_p, conv_p = _trunk(
        x_prompt, mods_p, per_batch(kv_mod_all, 2),
        [gla_prompt(i) for i in range(n_a)], zero_taps, None, kvq_prompt,
        lambda ctx: _moba_prompt(*ctx, hd), w, tiles_p)

    n_rows = bs * ts
    xs = x_sample.reshape(1, n_rows, d)
    pos_s = past_len + jnp.arange(ts, dtype=jnp.int32)

    def gla_sample(i):
        s0 = jnp.swapaxes(state_gla[i], -1, -2)
        sh_m, sc_m = [m[bp:bp + bs, j * d:(j + 1) * d][:, None, :] for m in [mod_all[i]]
                      for j in range(2)]

        def run(x):
            xp = _pad_rows(x.reshape(bs, ts, d), GLA_CHUNK)
            o, s_fin = _gla(xp, sh_m, sc_m, *gla_weights(i), s0, norm_g[i], heads, dk, dv,
                            GLA_CHUNK, ts)
            return o[:, :ts].reshape(1, n_rows, heads * dv), s_fin
        return run

    qpad = 8
    cache_k_t = jnp.transpose(cache_k, (0, 2, 3, 1))
    cache_v_t = jnp.transpose(cache_v, (0, 2, 3, 1))
    tables_s = _rope_tables(jnp.tile(pos_s, bs), hd)
    tiles_s = {"row": n_rows, "ffn": n_rows}

    def kvq_sample(x, *mods):
        k, v, q = _kvq_decode(x, mods, w_k, w_v, w_q, tables_s, hd, tiles_s["row"])
        return k, v, (q, k, v)

    def attend_sample(ctx):
        q, k, v = ctx
        seqs = lambda a, n: _pad_rows(a.reshape(bs, ts, dm), n)
        head_mask = (jnp.arange(dm)[None, :] // hd == jnp.arange(m_heads)[:, None]).astype(F32)
        qm = (q.reshape(bs, 1, ts, dm) * head_mask[None, :, None, :]).reshape(bs, m_heads * ts, dm)
        o = _moba_decode(page_table, qm, seqs(k, LANES), seqs(v, LANES), cache_k_t, cache_v_t,
                         ts, qpad)
        return o[:, :ts].reshape(1, n_rows, dm)

    def taps(i):
        st = state_ffn_conv[i]
        zeros = jnp.zeros((bs, ts - 1, ff), F32)
        tap1 = jnp.concatenate([st[:, 1:2], zeros], axis=1)
        tap2 = jnp.concatenate([st[:, 0:1], st[:, 1:2], zeros[:, :ts - 2]], axis=1)
        return [tap1.reshape(1, n_rows, ff), tap2.reshape(1, n_rows, ff)]

    y_s, k_s, v_s, gla_s, conv_s = _trunk(
        xs, [per_token(m, 6) for m in mod_all], per_token(kv_mod_all, 2),
        [gla_sample(i) for i in range(n_a)], [taps(i) for i in range(depth)], ts, kvq_sample,
        attend_sample, w, tiles_s)

    gla_state_p = jnp.stack([jnp.swapaxes(s, -1, -2) for s in gla_p])
    gla_state_s = jnp.stack([jnp.swapaxes(s, -1, -2) for s in gla_s])
    conv_state_p = jnp.stack(conv_p)
    conv_state_s = jnp.stack([g.reshape(bs, ts, ff)[:, ts - 2:] for g in conv_s])
    return (y_p, y_s.reshape(bs, ts, d),
            jnp.transpose(k_p, (0, 3, 1, 2)), jnp.transpose(v_p, (0, 3, 1, 2)),
            k_s.reshape(bs, ts, m_heads, hd), v_s.reshape(bs, ts, m_heads, hd),
            gla_state_p, gla_state_s, conv_state_p, conv_state_s)
```

```python
import functools

import jax
import jax.numpy as jnp
from jax import lax
from jax.experimental import pallas as pl
from jax.experimental.pallas import tpu as pltpu

F32 = jnp.float32
BF = jnp.bfloat16

LN_EPS = 1e-5
GLA_CHUNK = 64
GLA_GATE_NORM = 16.0
GLA_GATE_RANK_PAD = 128
MOBA_BLOCK = 256
MOBA_TOPK = 3
ROPE_THETA = 10000.0
MASKED = -1e30
LOG2_E = 1.4426950408889634
SUM_ROWS = 16
LANES = 128
VMEM_LIMIT = 56 * 1024 * 1024


def _params(*sem):
    return pltpu.CompilerParams(dimension_semantics=sem, vmem_limit_bytes=VMEM_LIMIT)


def _dot(a, b):
    return jnp.dot(a, b, preferred_element_type=F32)


def _dot_nt(a, b):
    return lax.dot_general(a, b, (((1,), (1,)), ((), ())), preferred_element_type=F32)


def _dot_tn(a, b):
    return lax.dot_general(a, b, (((0,), (0,)), ((), ())), preferred_element_type=F32)


def _layer_norm(z, g, b):
    mu = jnp.mean(z, axis=-1, keepdims=True)
    zc = z - mu
    var = jnp.mean(zc * zc, axis=-1, keepdims=True)
    return zc * lax.rsqrt(var + LN_EPS) * g + b


def _const_spec(shape):
    nd = len(shape)
    return pl.BlockSpec(shape, lambda *_: (0,) * nd, pipeline_mode=pl.Buffered(1))


def _mod_spec(arr, tm):
    if arr.shape[1] == 1:
        return pl.BlockSpec((1, 1, arr.shape[2]), lambda b, t: (b, 0, 0))
    return pl.BlockSpec((1, tm, arr.shape[2]), lambda b, t: (b, t, 0))


def _row_spec(tm, width):
    return pl.BlockSpec((1, tm, width), lambda b, t: (b, t, 0))


def _ada_kernel(c_ref, w_ref, b_ref, o_ref):
    c = c_ref[...]
    s = c * jax.nn.sigmoid(c)
    o_ref[...] = _dot(s.astype(BF), w_ref[...].astype(BF)) + b_ref[...]


def _ada(c, w3, layer, b2, tn=1024):
    r, d = c.shape
    n = w3.shape[2]
    return pl.pallas_call(
        _ada_kernel,
        out_shape=jax.ShapeDtypeStruct((r, n), F32),
        grid=(n // tn,),
        in_specs=[pl.BlockSpec((r, d), lambda j: (0, 0)),
                  pl.BlockSpec((None, d, tn), lambda j: (layer, 0, j)),
                  pl.BlockSpec((1, tn), lambda j: (0, j))],
        out_specs=pl.BlockSpec((r, tn), lambda j: (0, j)),
        compiler_params=_params("arbitrary"),
        name="ada_mod",
    )(c, w3, b2)


def _split3(x):
    hi = x.astype(BF)
    r = x - hi.astype(F32)
    mid = r.astype(BF)
    lo = (r - mid.astype(F32)).astype(BF)
    return hi, mid, lo


def _gla_kernel(x_ref, sh_ref, sc_ref, win_ref, wg1_ref, wg2_ref, bgk_ref, s0_ref, ng_ref,
                o_ref, sfin_ref, s_scr, q_scr, k_scr, v_scr, g_scr, gk_scr,
                *, heads, dk, dv, chunk, n_valid, scale):
    t = pl.program_id(1)
    tm = x_ref.shape[1]
    kd, vd = heads * dk, heads * dv

    @pl.when(t == 0)
    def _():
        s_scr[...] = s0_ref[0]

    h = (x_ref[0] * (1.0 + sc_ref[0]) + sh_ref[0]).astype(BF)
    q_scr[...] = _dot(h, win_ref[:, 0:kd]) * scale
    k = _dot(h, win_ref[:, kd:2 * kd])
    v_scr[...] = _dot(h, win_ref[:, 2 * kd:2 * kd + vd]).astype(BF)
    g_scr[...] = _dot(h, win_ref[:, 2 * kd + vd:2 * kd + 2 * vd])
    lin = _dot(_dot(h, wg1_ref[...]).astype(BF), wg2_ref[...]) + bgk_ref[...]
    gk = (jnp.minimum(lin, 0.0) - jnp.log1p(jnp.exp(-jnp.abs(lin)))) * (1.0 / GLA_GATE_NORM)
    if n_valid < tm:
        valid = lax.broadcasted_iota(jnp.int32, k.shape, 0) < n_valid
        k = jnp.where(valid, k, 0.0)
        gk = jnp.where(valid, gk, 0.0)
    k_scr[...] = k
    gk_scr[...] = gk

    row = lax.broadcasted_iota(jnp.int32, (chunk, chunk), 0)
    col = lax.broadcasted_iota(jnp.int32, (chunk, chunk), 1)
    tril_f = (row >= col).astype(F32)
    tril = tril_f.astype(BF)
    ng = ng_ref[...]

    for c in range(tm // chunk):
        rows = pl.ds(c * chunk, chunk)
        hi, mid, lo = _split3(gk_scr[rows, :])
        cum = _dot(tril, hi) + _dot(tril, mid) + _dot(tril, lo)
        last = cum[chunk - 1:chunk, :]
        q_in = (q_scr[rows, :] * jnp.exp(cum)).astype(BF)
        kc = k_scr[rows, :]
        k_in = (kc * jnp.exp(-cum)).astype(BF)
        k_end = (kc * jnp.exp(last - cum)).astype(BF)
        dec = jnp.exp(last)
        for hh in range(heads):
            ks = slice(hh * dk, (hh + 1) * dk)
            vs = slice(hh * dv, (hh + 1) * dv)
            a = _dot_nt(q_in[:, ks], k_in[:, ks]) * tril_f
            vh = v_scr[rows, vs]
            st = s_scr[hh]
            o = _dot(a.astype(BF), vh) + _dot_nt(q_in[:, ks], st.astype(BF))
            s_scr[hh] = st * dec[:, ks] + _dot_tn(vh, k_end[:, ks])
            o = o * lax.rsqrt(jnp.mean(o * o, axis=-1, keepdims=True) + LN_EPS) * ng
            gh = g_scr[rows, vs]
            o_ref[0, rows, vs] = (o * (gh * jax.nn.sigmoid(gh))).astype(BF)

    @pl.when(t == pl.num_programs(1) - 1)
    def _():
        sfin_ref[0] = s_scr[...]


def _gla(x, sh, sc, w_in, wg1, wg2, bgk, s0t, norm_g, heads, dk, dv, tm, n_valid):
    b, t, d = x.shape
    kd, vd = heads * dk, heads * dv
    kern = functools.partial(_gla_kernel, heads=heads, dk=dk, dv=dv, chunk=GLA_CHUNK,
                             n_valid=n_valid, scale=float(dk) ** -0.5)
    st_spec = pl.BlockSpec((1, heads, dv, dk), lambda bb, tt: (bb, 0, 0, 0))
    return pl.pallas_call(
        kern,
        out_shape=[jax.ShapeDtypeStruct((b, t, vd), BF),
                   jax.ShapeDtypeStruct((b, heads, dv, dk), F32)],
        grid=(b, t // tm),
        in_specs=[_row_spec(tm, d), _mod_spec(sh, tm), _mod_spec(sc, tm),
                  _const_spec(w_in.shape), _const_spec(wg1.shape), _const_spec(wg2.shape),
                  _const_spec(bgk.shape), st_spec, _const_spec(norm_g.shape)],
        out_specs=[_row_spec(tm, vd), st_spec],
        scratch_shapes=[pltpu.VMEM((heads, dv, dk), F32), pltpu.VMEM((tm, kd), F32),
                        pltpu.VMEM((tm, kd), F32), pltpu.VMEM((tm, vd), BF),
                        pltpu.VMEM((tm, vd), F32), pltpu.VMEM((tm, kd), F32)],
        compiler_params=_params("parallel", "arbitrary"),
        name="gla",
    )(x, sh, sc, w_in, wg1, wg2, bgk, s0t, norm_g)


def _ffn_kernel(*refs, alpha, ff, fc, tm, seq_len):
    x_ref, o_ref, gm_ref, wout_ref, lng0_ref, lnb0_ref, sh_ref, sc_ref, gate_ref = refs[:9]
    if seq_len is None:
        (st_ref, wup_ref, wconv_ref, bconv_ref, wdown_ref,
         lng_ref, lnb_ref, out_ref, tail_ref, act, carry) = refs[9:]
    else:
        (tap1_ref, tap2_ref, wup_ref, wconv_ref, bconv_ref,
         wdown_ref, lng_ref, lnb_ref, out_ref, tail_ref, act) = refs[9:]
    t = pl.program_id(1)
    x = _layer_norm(alpha * x_ref[0] + (1.0 + gm_ref[0]) * _dot(o_ref[0], wout_ref[...]),
                    lng0_ref[...], lnb0_ref[...])
    h = (x * (1.0 + sc_ref[0]) + sh_ref[0]).astype(BF)
    rowi = lax.broadcasted_iota(jnp.int32, (tm, fc), 0)
    for c in range(ff // fc):
        cs = slice(c * fc, (c + 1) * fc)
        u = _dot(h, wup_ref[:, c * fc:(c + 1) * fc])
        g = _dot(h, wup_ref[:, ff + c * fc:ff + (c + 1) * fc])
        g1 = pltpu.roll(g, 1, 0)
        g2 = pltpu.roll(g, 2, 0)
        if seq_len is None:
            prev = jnp.where(t == 0, st_ref[0, :, cs], carry[0:2, cs])
            g1 = jnp.where(rowi == 0, prev[1:2, :], g1)
            g2 = jnp.where(rowi == 0, prev[0:1, :], jnp.where(rowi == 1, prev[1:2, :], g2))
            carry[0:2, cs] = g[tm - 2:tm, :]
            tail_ref[0, :, cs] = g[tm - 2:tm, :]
        else:
            pos = rowi & (seq_len - 1)
            g1 = jnp.where(pos >= 1, g1, tap1_ref[0, :, cs])
            g2 = jnp.where(pos >= 2, g2, tap2_ref[0, :, cs])
            tail_ref[0, :, cs] = g
        gc = (bconv_ref[:, cs] + g2 * wconv_ref[0:1, cs] + g1 * wconv_ref[1:2, cs]
              + g * wconv_ref[2:3, cs])
        act[:, cs] = (0.5 * gc * (1.0 + lax.erf(gc * 0.7071067811865476)) * u).astype(BF)
    z = alpha * x + (1.0 + gate_ref[0]) * _dot(act[...], wdown_ref[...])
    out_ref[0] = _layer_norm(z, lng_ref[...], lnb_ref[...])


def _ffn(x, o, gate_m, w_out, lng0, lnb0, sh, sc, gate, taps, w_up, w_conv, b_conv, w_down, lng,
         lnb, alpha, tm, seq_len):
    b, t, d = x.shape
    ff = w_down.shape[0]
    fc = 256 if ff % 256 == 0 else LANES
    kern = functools.partial(_ffn_kernel, alpha=alpha, ff=ff, fc=fc, tm=tm, seq_len=seq_len)
    wspecs = [_const_spec(w_up.shape), _const_spec(w_conv.shape), _const_spec(b_conv.shape),
              _const_spec(w_down.shape), _const_spec(lng.shape), _const_spec(lnb.shape)]
    mixer_specs = [_row_spec(tm, d), _row_spec(tm, o.shape[2]), _mod_spec(gate_m, tm),
                   _const_spec(w_out.shape), _const_spec(lng0.shape), _const_spec(lnb0.shape)]
    if seq_len is None:
        tap_specs = [pl.BlockSpec((1, 2, ff), lambda bb, tt: (bb, 0, 0))]
        tail_shape = jax.ShapeDtypeStruct((b, 2, ff), F32)
        tail_spec = pl.BlockSpec((1, 2, ff), lambda bb, tt: (bb, 0, 0))
        scratch = [pltpu.VMEM((tm, ff), BF), pltpu.VMEM((8, ff), F32)]
        sem = ("parallel", "arbitrary")
    else:
        assert seq_len & (seq_len - 1) == 0 and tm % seq_len == 0
        tap_specs = [_row_spec(tm, ff), _row_spec(tm, ff)]
        tail_shape = jax.ShapeDtypeStruct((b, t, ff), F32)
        tail_spec = _row_spec(tm, ff)
        scratch = [pltpu.VMEM((tm, ff), BF)]
        sem = ("parallel", "parallel")
    return pl.pallas_call(
        kern,
        out_shape=[jax.ShapeDtypeStruct((b, t, d), F32), tail_shape],
        grid=(b, t // tm),
        in_specs=mixer_specs + [_mod_spec(sh, tm), _mod_spec(sc, tm), _mod_spec(gate, tm)]
        + tap_specs + wspecs,
        out_specs=[_row_spec(tm, d), tail_spec],
        scratch_shapes=scratch,
        compiler_params=_params(*sem),
        name="conv_ffn",
    )(x, o, gate_m, w_out, lng0, lnb0, sh, sc, gate, *taps, w_up, w_conv, b_conv, w_down, lng, lnb)


def _rope_rows(x, cos, sin, hd):
    half = hd // 2
    lane = lax.broadcasted_iota(jnp.int32, (x.shape[0], LANES), 1)
    first = (lane & (hd - 1)) < half
    out = []
    for s in range(x.shape[1] // LANES):
        xs = x[:, s * LANES:(s + 1) * LANES]
        partner = jnp.where(first, pltpu.roll(xs, LANES - half, 1), pltpu.roll(xs, half, 1))
        out.append(xs * cos + partner * sin)
    return out


def _rope_cols(x3, cos_t, sin_t):
    half = x3.shape[1] // 2
    partner = jnp.concatenate([x3[:, half:, :], x3[:, :half, :]], axis=1)
    return x3 * cos_t[None] + partner * sin_t[None]


def _kvq_decode_kernel(x_ref, shk_ref, sck_ref, shq_ref, scq_ref, wk_ref, wv_ref, wq_ref,
                       cos_ref, sin_ref, k_ref, v_ref, q_ref, *, hd):
    x = x_ref[0]
    hk = (x * (1.0 + sck_ref[0]) + shk_ref[0]).astype(BF)
    hq = (x * (1.0 + scq_ref[0]) + shq_ref[0]).astype(BF)
    cos = cos_ref[...]
    sin = sin_ref[...]
    for s, piece in enumerate(_rope_rows(_dot(hk, wk_ref[...]), cos, sin, hd)):
        k_ref[0, :, s * LANES:(s + 1) * LANES] = piece
    v_ref[0] = _dot(hk, wv_ref[...])
    for s, piece in enumerate(_rope_rows(_dot(hq, wq_ref[...]), cos, sin, hd)):
        q_ref[0, :, s * LANES:(s + 1) * LANES] = piece


def _kvq_prompt_kernel(x_ref, shk_ref, sck_ref, shq_ref, scq_ref, wk_ref, wkt_ref, wvt_ref,
                       wqt_ref, cos_ref, sin_ref, cos_t_ref, sin_t_ref,
                       kt_ref, vt_ref, krow_ref, vtb_ref, qt_ref, km_ref, *, hd, blk):
    x = x_ref[0]
    tm = x.shape[0]
    hk = (x * (1.0 + sck_ref[0]) + shk_ref[0]).astype(BF)
    hq = (x * (1.0 + scq_ref[0]) + shq_ref[0]).astype(BF)
    heads = kt_ref.shape[1]
    cos_t = cos_t_ref[...]
    sin_t = sin_t_ref[...]
    kt_ref[0] = _rope_cols(_dot_nt(wkt_ref[...], hk).reshape(heads, hd, tm), cos_t, sin_t)
    vt = _dot_nt(wvt_ref[...], hk).reshape(heads, hd, tm)
    vt_ref[0] = vt
    vtb_ref[0, :, 0:hd, :] = vt.astype(BF)
    vtb_ref[0, :, hd:, :] = jnp.ones((heads, vtb_ref.shape[2] - hd, tm), BF)
    qt_ref[0] = _rope_cols(_dot_nt(wqt_ref[...], hq).reshape(heads, hd, tm), cos_t,
                           sin_t).reshape(heads * hd, tm)
    for s, piece in enumerate(_rope_rows(_dot(hk, wk_ref[...]), cos_ref[...], sin_ref[...], hd)):
        cols = slice(s * LANES, (s + 1) * LANES)
        krow_ref[0, :, cols] = piece.astype(BF)
        for j in range(tm // blk):
            km_ref[0, 0, j:j + 1, cols] = jnp.mean(piece[j * blk:(j + 1) * blk, :], axis=0,
                                                   keepdims=True)


def _kvq_specs(x, mods, tm):
    return [_row_spec(tm, x.shape[2])] + [_mod_spec(m, tm) for m in mods]


def _kvq_decode(x, mods, wk, wv, wq, tables, hd, tm):
    b, t, d = x.shape
    dm = wq.shape[1]
    cos, sin = tables[:2]
    tab_spec = pl.BlockSpec((tm, LANES), lambda bb, tt: (tt, 0))
    row = jax.ShapeDtypeStruct((b, t, dm), F32)
    return pl.pallas_call(
        functools.partial(_kvq_decode_kernel, hd=hd),
        out_shape=[row, row, row],
        grid=(b, t // tm),
        in_specs=_kvq_specs(x, mods, tm) + [_const_spec(wk.shape), _const_spec(wv.shape),
                                            _const_spec(wq.shape), tab_spec, tab_spec],
        out_specs=[_row_spec(tm, dm)] * 3,
        compiler_params=_params("parallel", "parallel"),
        name="kvq_decode",
    )(x, *mods, wk, wv, wq, cos, sin)


def _kvq_prompt(x, mods, wk, wv, wq, tables, hd, tm):
    b, t, d = x.shape
    dm = wq.shape[1]
    heads = dm // hd
    blk = MOBA_BLOCK
    cos, sin, cos_t, sin_t = tables
    tab_spec = pl.BlockSpec((tm, LANES), lambda bb, tt: (tt, 0))
    tab_t_spec = pl.BlockSpec((hd, tm), lambda bb, tt: (0, tt))
    col_spec = pl.BlockSpec((1, heads, hd, tm), lambda bb, tt: (bb, 0, 0, tt))
    w_spec = _const_spec(wk.shape)
    return pl.pallas_call(
        functools.partial(_kvq_prompt_kernel, hd=hd, blk=blk),
        out_shape=[jax.ShapeDtypeStruct((b, heads, hd, t), F32),
                   jax.ShapeDtypeStruct((b, heads, hd, t), F32),
                   jax.ShapeDtypeStruct((b, t, dm), BF),
                   jax.ShapeDtypeStruct((b, heads, hd + SUM_ROWS, t), BF),
                   jax.ShapeDtypeStruct((b, dm, t), F32),
                   jax.ShapeDtypeStruct((b, t // tm, tm // blk, dm), F32)],
        grid=(b, t // tm),
        in_specs=_kvq_specs(x, mods, tm) + [w_spec, w_spec, w_spec, w_spec, tab_spec, tab_spec,
                                            tab_t_spec, tab_t_spec],
        out_specs=[col_spec, col_spec, _row_spec(tm, dm),
                   pl.BlockSpec((1, heads, hd + SUM_ROWS, tm), lambda bb, tt: (bb, 0, 0, tt)),
                   pl.BlockSpec((1, dm, tm), lambda bb, tt: (bb, 0, tt)),
                   pl.BlockSpec((1, 1, tm // blk, dm), lambda bb, tt: (bb, tt, 0, 0))],
        compiler_params=_params("parallel", "parallel"),
        name="kvq_prompt",
    )(x, *mods, wk, wk.T, wv.T, wq.T, cos, sin, cos_t, sin_t)


def _rope_tables(pos, hd):
    half = hd // 2
    inv = ROPE_THETA ** (-jnp.arange(half, dtype=F32) / half)
    ang = pos.astype(F32)[:, None] * inv[None, :]
    cos, sin = jnp.cos(ang), jnp.sin(ang)
    cos_h = jnp.concatenate([cos, cos], axis=1)
    sin_h = jnp.concatenate([-sin, sin], axis=1)
    reps = LANES // hd
    return jnp.tile(cos_h, (1, reps)), jnp.tile(sin_h, (1, reps)), cos_h.T, sin_h.T


def _select_bias(gate, n_valid, topk, axis):
    bidx = lax.broadcasted_iota(jnp.int32, gate.shape, axis)
    bidx_f = bidx.astype(F32)
    ninf = jnp.float32(-jnp.inf)
    gate = jnp.where(bidx < n_valid, gate, ninf)
    sel = jnp.zeros(gate.shape, jnp.bool_)
    for _ in range(topk):
        mx = jnp.max(gate, axis=axis, keepdims=True)
        first = jnp.min(jnp.where(gate == mx, bidx_f, jnp.float32(gate.shape[axis])), axis=axis,
                        keepdims=True)
        pick = (bidx_f == first) & (mx > ninf)
        sel = sel | pick
        gate = jnp.where(pick, ninf, gate)
    return jnp.where(sel, 0.0, MASKED)


def _gate_scores(q, km):
    return jnp.dot(q, km, preferred_element_type=F32, precision=lax.Precision.HIGHEST)


def _head_rows(qt, hh, hd):
    rowi = lax.broadcasted_iota(jnp.int32, qt.shape, 0)
    return jnp.where((rowi >= hh * hd) & (rowi < (hh + 1) * hd), qt, 0.0)


def _moba_gate_kernel(qt_ref, km_ref, bias_ref, *, hd, blk, topk):
    tq = qt_ref.shape[2]
    km = km_ref[0]
    nb = km.shape[0]
    pos = pl.program_id(2) * tq + lax.broadcasted_iota(jnp.int32, (1, tq), 1)
    own = pos >> (blk.bit_length() - 1)
    lane = lax.broadcasted_iota(jnp.int32, km.shape, 1)
    per = LANES // hd
    km_heads = jnp.concatenate(
        [jnp.where((lane >= hh * hd) & (lane < (hh + 1) * hd), km, 0.0) for hh in range(per)],
        axis=0)
    gate = _gate_scores(km_heads, qt_ref[0])
    for hh in range(per):
        bias_ref[0, hh] = _select_bias(gate[hh * nb:(hh + 1) * nb, :], own, topk, 0)


def _moba_gate(q_t, kmean, hd, tq):
    b, dm, t = q_t.shape
    n_blocks = kmean.shape[1]
    n_heads = LANES // hd
    return pl.pallas_call(
        functools.partial(_moba_gate_kernel, hd=hd, blk=MOBA_BLOCK, topk=MOBA_TOPK),
        out_shape=jax.ShapeDtypeStruct((b, dm // hd, n_blocks, t), F32),
        grid=(b, dm // LANES, t // tq),
        in_specs=[pl.BlockSpec((1, LANES, tq), lambda bb, hp, tt: (bb, hp, tt)),
                  pl.BlockSpec((1, n_blocks, LANES), lambda bb, hp, tt: (bb, 0, hp))],
        out_specs=pl.BlockSpec((1, n_heads, n_blocks, tq), lambda bb, hp, tt: (bb, hp, 0, tt)),
        compiler_params=_params("parallel", "parallel", "parallel"),
        name="moba_gate",
    )(q_t, kmean)


def _moba_prompt_kernel(qt_ref, k_ref, vt_ref, bias_ref, o_ref, qb_scr, acc_scr,
                        *, hd, blk, kb, scale):
    i = pl.program_id(2)
    per = LANES // hd
    n_heads = vt_ref.shape[1]
    heads = range(n_heads)
    key = lax.broadcasted_iota(jnp.int32, (blk, blk), 0)
    qry = lax.broadcasted_iota(jnp.int32, (blk, blk), 1)
    own = pl.multiple_of(i * blk, blk)

    def k_tile(hh, start):
        g = hh // per
        return k_ref[0, pl.ds(start, blk), g * LANES:(g + 1) * LANES]

    for hh in heads:
        g = hh // per
        qh = _head_rows(qt_ref[0, g * LANES:(g + 1) * LANES, :], hh % per, hd)
        qb_scr[hh] = (qh * (scale * LOG2_E)).astype(BF)
    own_s = [jnp.where(key <= qry, _dot(k_tile(hh, own), qb_scr[hh]), MASKED) for hh in heads]
    maxes = [jnp.max(s, axis=0, keepdims=True) for s in own_s]
    own_p = [jnp.exp2(s - m).astype(BF) for s, m in zip(own_s, maxes)]
    for hh in heads:
        acc_scr[hh] = _dot(vt_ref[0, hh, :, pl.ds(own, blk)], own_p[hh])

    def body(it, carry):
        tiles = [(hh, b, pl.multiple_of((it * kb + b) * blk, blk))
                 for b in range(kb) for hh in heads]
        scores = [_dot(k_tile(hh, start), qb_scr[hh]) for hh, b, start in tiles]
        parts = [[] for _ in heads]
        probs = []
        for (hh, b, start), s in zip(tiles, scores):
            cm = jnp.max(s, axis=0, keepdims=True)
            probs.append(jnp.exp2(s - cm).astype(BF))
            parts[hh].append([cm + bias_ref[0, hh, pl.ds(it * kb + b, 1), :]])
        for (hh, b, start), p in zip(tiles, probs):
            parts[hh][b].append(_dot(vt_ref[0, hh, :, pl.ds(start, blk)], p))
        new = []
        for hh in heads:
            m_prev = carry[hh]
            m_new = m_prev
            for m_b, _ in parts[hh]:
                m_new = jnp.maximum(m_new, m_b)
            acc = jnp.exp2(m_prev - m_new) * acc_scr[hh]
            for m_b, acc_b in parts[hh]:
                acc = acc + jnp.exp2(m_b - m_new) * acc_b
            acc_scr[hh] = acc
            new.append(m_new)
        return tuple(new)

    lax.fori_loop(0, (i + kb - 1) // kb, body, tuple(maxes))
    for g in range(n_heads // per):
        out_t = jnp.concatenate([acc_scr[hh, 0:hd, :] / acc_scr[hh, hd:hd + 1, :]
                                 for hh in range(g * per, (g + 1) * per)], axis=0)
        o_ref[0, :, g * LANES:(g + 1) * LANES] = out_t.T.astype(o_ref.dtype)


def _moba_prompt(q_t, k, v_t, kmean, hd):
    b, dm, t = q_t.shape
    blk = MOBA_BLOCK
    n_blocks = t // blk
    kb = 2 if n_blocks % 2 == 0 else 1
    width = 4 * LANES if dm % (4 * LANES) == 0 else LANES
    n_heads = width // hd
    bias = _moba_gate(q_t, kmean, hd, min(t, 1024))
    kern = functools.partial(_moba_prompt_kernel, hd=hd, blk=blk, kb=kb, scale=float(hd) ** -0.5)
    return pl.pallas_call(
        kern,
        out_shape=jax.ShapeDtypeStruct((b, t, dm), BF),
        grid=(b, dm // width, n_blocks),
        in_specs=[pl.BlockSpec((1, width, blk), lambda bb, hg, i: (bb, hg, i)),
                  pl.BlockSpec((1, t, width), lambda bb, hg, i: (bb, 0, hg)),
                  pl.BlockSpec((1, n_heads, v_t.shape[2], t), lambda bb, hg, i: (bb, hg, 0, 0)),
                  pl.BlockSpec((1, n_heads, n_blocks, blk), lambda bb, hg, i: (bb, hg, 0, i))],
        out_specs=pl.BlockSpec((1, blk, width), lambda bb, hg, i: (bb, i, hg)),
        scratch_shapes=[pltpu.VMEM((n_heads, LANES, blk), BF),
                        pltpu.VMEM((n_heads, v_t.shape[2], blk), F32)],
        compiler_params=_params("parallel", "parallel", "arbitrary"),
        name="moba_prompt",
    )(q_t, k, v_t, bias)


def _moba_decode_kernel(*refs, hd, ts, bps, n_new, n_blocks, topk, scale):
    n_pg = 2 * bps
    pt_ref, qm_ref, kn_ref, vn_ref = refs[:4]
    k_refs = refs[4:4 + n_pg]
    v_refs = refs[4 + n_pg:4 + 2 * n_pg]
    o_ref, qb_scr, km_scr, m_scr, l_scr, acc_all = refs[4 + 2 * n_pg:]
    j = pl.program_id(1)
    rows, dm = qm_ref.shape[1], qm_ref.shape[2]
    page = k_refs[0].shape[3]
    blk = 2 * page
    col_km = lax.broadcasted_iota(jnp.int32, km_scr.shape, 1)
    col_st = lax.broadcasted_iota(jnp.int32, m_scr.shape, 1)
    row_h = lax.broadcasted_iota(jnp.int32, (rows, dm), 0) >> (ts.bit_length() - 1)
    lane_h = lax.broadcasted_iota(jnp.int32, (rows, dm), 1) >> (hd.bit_length() - 1)

    def own_head_lanes(x):
        x = jnp.where(row_h == lane_h, x, 0.0)
        out = x[:, 0:LANES]
        for g in range(1, dm // LANES):
            out = out + x[:, g * LANES:(g + 1) * LANES]
        return out

    @pl.when(j == 0)
    def _():
        qb_scr[...] = (qm_ref[0] * (scale * LOG2_E)).astype(BF)
        km_scr[...] = jnp.zeros_like(km_scr)
        m_scr[...] = jnp.full(m_scr.shape, MASKED, F32)
        l_scr[...] = jnp.zeros_like(l_scr)

    qb = qb_scr[...]
    for b in range(bps):
        idx = j * bps + b
        ke = k_refs[2 * b][0].reshape(dm, page)
        ko = k_refs[2 * b + 1][0].reshape(dm, page)
        mean = jnp.sum(ke + ko, axis=-1, keepdims=True) * (1.0 / blk)
        km_scr[...] = jnp.where(col_km == idx, mean, km_scr[...])
        s = jnp.concatenate([_dot(qb, ke.astype(BF)), _dot(qb, ko.astype(BF))], axis=-1)
        cm = jnp.max(s, axis=-1, keepdims=True)
        p = jnp.exp2(s - cm)
        m_scr[...] = jnp.where(col_st == idx, cm, m_scr[...])
        l_scr[...] = jnp.where(col_st == idx, jnp.sum(p, axis=-1, keepdims=True), l_scr[...])
        pb = p.astype(BF)
        acc_all[idx] = own_head_lanes(
            _dot_nt(pb[:, :page], v_refs[2 * b][0].reshape(dm, page).astype(BF))
            + _dot_nt(pb[:, page:], v_refs[2 * b + 1][0].reshape(dm, page).astype(BF)))

    @pl.when(j == pl.num_programs(1) - 1)
    def _():
        gate = _gate_scores(qm_ref[0], km_scr[...])
        m_blk = m_scr[...] + _select_bias(gate, n_blocks, topk, 1)
        s = _dot_nt(qb, kn_ref[0].astype(BF))
        qpos = lax.broadcasted_iota(jnp.int32, s.shape, 0) & (ts - 1)
        kpos = lax.broadcasted_iota(jnp.int32, s.shape, 1)
        s = jnp.where((kpos <= qpos) & (kpos < n_new), s, MASKED)
        m_own = jnp.max(s, axis=-1, keepdims=True)
        p = jnp.exp2(s - m_own)
        m_all = jnp.maximum(m_own, jnp.max(m_blk, axis=-1, keepdims=True))
        w_own = jnp.exp2(m_own - m_all)
        w_blk = jnp.exp2(m_blk - m_all)
        l_all = (w_own * jnp.sum(p, axis=-1, keepdims=True)
                 + jnp.sum(w_blk * l_scr[...], axis=-1, keepdims=True))
        acc0 = w_own * own_head_lanes(_dot(p.astype(BF), vn_ref[0].astype(BF)))

        accs = [acc0] + [jnp.zeros_like(acc0)] * 3
        for jb in range(n_blocks):
            accs[jb % 4] = accs[jb % 4] + w_blk[:, jb:jb + 1] * acc_all[jb]
        res = ((accs[0] + accs[1] + (accs[2] + accs[3])) / l_all).astype(BF)
        qrow = lax.broadcasted_iota(jnp.int32, (o_ref.shape[1], rows), 0)
        r = lax.broadcasted_iota(jnp.int32, (o_ref.shape[1], rows), 1)
        r_group = r >> ((ts * (LANES // hd)).bit_length() - 1)
        for g in range(dm // LANES):
            pick = ((r & (ts - 1)) == qrow) & (r_group == g)
            o_ref[0, :, g * LANES:(g + 1) * LANES] = _dot(pick.astype(BF), res).astype(o_ref.dtype)


def _moba_decode(page_table, qm, k_new, v_new, cache_k_t, cache_v_t, ts, qpad):
    s, rows, dm = qm.shape
    _, n_heads, hd, page = cache_k_t.shape
    n_blocks = page_table.shape[1] // 2
    bps = 8 if n_blocks % 8 == 0 else 1
    assert 2 * page == MOBA_BLOCK and n_blocks <= LANES and rows == n_heads * ts
    kern = functools.partial(_moba_decode_kernel, hd=hd, ts=ts, bps=bps, n_new=ts,
                             n_blocks=n_blocks, topk=MOBA_TOPK, scale=float(hd) ** -0.5)

    def page_spec(k):
        return pl.BlockSpec((1, n_heads, hd, page),
                            lambda ss, j, pt: (pt[ss, 2 * bps * j + k], 0, 0, 0))

    pages = [page_spec(k) for k in range(2 * bps)]
    per_seq = lambda a: pl.BlockSpec((1,) + a.shape[1:], lambda ss, j, pt: (ss, 0, 0))
    grid_spec = pltpu.PrefetchScalarGridSpec(
        num_scalar_prefetch=1,
        grid=(s, n_blocks // bps),
        in_specs=[per_seq(qm), per_seq(k_new), per_seq(v_new)] + pages + pages,
        out_specs=pl.BlockSpec((1, qpad, dm), lambda ss, j, pt: (ss, 0, 0)),
        scratch_shapes=[pltpu.VMEM((rows, dm), BF),
                        pltpu.VMEM((dm, LANES), F32),
                        pltpu.VMEM((rows, LANES), F32),
                        pltpu.VMEM((rows, LANES), F32),
                        pltpu.VMEM((n_blocks, rows, LANES), F32)],
    )
    return pl.pallas_call(
        kern,
        out_shape=jax.ShapeDtypeStruct((s, qpad, dm), BF),
        grid_spec=grid_spec,
        compiler_params=_params("parallel", "arbitrary"),
        name="moba_decode",
    )(page_table, qm, k_new, v_new, *([cache_k_t] * (2 * bps)), *([cache_v_t] * (2 * bps)))


def _pad_rows(a, n, axis=1):
    pad = [(0, 0)] * a.ndim
    pad[axis] = (0, n - a.shape[axis])
    return jnp.pad(a, pad)


def _trunk(x, mods, kv_mod, gla_fns, conv_taps, seq_len, kvq_fn, attend, w, tiles):
    depth = len(mods)
    n_a = depth // 2
    alpha = float((2 * depth) ** 0.25)
    gla_states, conv_tails = [], []
    k_sh = v_sh = ctx = None
    for i in range(depth):
        sh_m, sc_m, g_m, sh_f, sc_f, g_f = mods[i]
        if i < n_a:
            o, s_fin = gla_fns[i](x)
            gla_states.append(s_fin)
            w_out = w["a_w_out"][i]
        else:
            o = attend(ctx)
            w_out = w["b_w_out"][i - n_a]
        x, tail = _ffn(x, o, g_m, w_out, w["ln_g"][i, 0:1], w["ln_b"][i, 0:1], sh_f, sc_f, g_f,
                       conv_taps[i], w["w_up"][i], w["w_conv"][i], w["b_conv"][i:i + 1],
                       w["w_down"][i], w["ln_g"][i, 1:2], w["ln_b"][i, 1:2], alpha, tiles["ffn"],
                       seq_len)
        conv_tails.append(tail)
        if i == n_a - 1:
            k_sh, v_sh, ctx = kvq_fn(x, kv_mod[0], kv_mod[1], mods[n_a][0], mods[n_a][1])
    return x, k_sh, v_sh, gla_states, conv_tails


def kernel(x_prompt, x_sample, cache_k, cache_v, state_gla, state_ffn_conv, page_table, c_prompt,
           c_sample, w_ada, b_ada, ln_g, ln_b, w_up, w_conv, b_conv, w_down, a_w_in, a_w_gk1,
           a_w_gk2, a_b_gk, a_norm_g, a_w_out, kv_w_ada, kv_b_ada, kv_w, b_w_q, b_w_out):
    bp, tp, d = x_prompt.shape
    bs, ts, _ = x_sample.shape
    depth = w_ada.shape[0]
    n_a = state_gla.shape[0]
    heads, dk, dv = state_gla.shape[2:]
    n_phys, page, m_heads, hd = cache_k.shape
    dm = m_heads * hd
    n_pages = page_table.shape[1]
    past_len = n_pages * page
    ff = w_down.shape[1]
    blk = MOBA_BLOCK
    assert depth == 2 * n_a == 2 and b_w_q.shape[0] == 1
    assert tp % blk == 0 and past_len % blk == 0
    assert 2 <= ts <= 8 and ts & (ts - 1) == 0
    assert LANES % hd == 0 and dm % LANES == 0

    w_k, w_v = kv_w[:, :dm].astype(BF), kv_w[:, dm:].astype(BF)
    w_q = b_w_q[0].astype(BF)
    per_layer = lambda a: [a[i].astype(BF) for i in range(a.shape[0])]
    w = {
        "a_w_in": per_layer(a_w_in),
        "a_w_gk1": _pad_rows(a_w_gk1, GLA_GATE_RANK_PAD, axis=2).astype(BF),
        "a_w_gk2": _pad_rows(a_w_gk2, GLA_GATE_RANK_PAD, axis=1).astype(BF),
        "a_b_gk": a_b_gk[:, None, :], "a_w_out": per_layer(a_w_out),
        "w_up": per_layer(w_up), "w_conv": w_conv, "b_conv": b_conv, "w_down": per_layer(w_down),
        "ln_g": ln_g, "ln_b": ln_b, "b_w_out": per_layer(b_w_out),
    }
    norm_g = [a_norm_g[i][None, :] for i in range(n_a)]

    n_c = bp + bs
    c_all = _pad_rows(jnp.concatenate([c_prompt, c_sample], axis=0), -(-n_c // 8) * 8, axis=0)
    mod_all = [_ada(c_all, w_ada, i, b_ada[i:i + 1]) for i in range(depth)]
    kv_mod_all = _ada(c_all, kv_w_ada[None], 0, kv_b_ada[None, :])

    def per_batch(m, n):
        return [m[:bp, j * d:(j + 1) * d][:, None, :] for j in range(n)]

    def per_token(m, n):
        return [jnp.repeat(m[bp:bp + bs, j * d:(j + 1) * d], ts, axis=0)[None] for j in range(n)]

    def gla_weights(i):
        return (w["a_w_in"][i], w["a_w_gk1"][i], w["a_w_gk2"][i], w["a_b_gk"][i])

    mods_p = [per_batch(m, 6) for m in mod_all]

    def gla_prompt(i):
        s0 = jnp.zeros((bp, heads, dv, dk), F32)
        return lambda x: _gla(x, mods_p[i][0], mods_p[i][1], *gla_weights(i), s0, norm_g[i],
                              heads, dk, dv, tiles_p["gla"], tiles_p["gla"])

    tables_p = _rope_tables(jnp.arange(tp, dtype=jnp.int32), hd)
    tiles_p = {"gla": 512, "row": 512, "ffn": 512}

    def kvq_prompt(x, *mods):
        k_t, v_t, k_row, v_tb, q_t, kmean = _kvq_prompt(x, mods, w_k, w_v, w_q, tables_p, hd,
                                                        tiles_p["row"])
        return k_t, v_t, (q_t, k_row, v_tb, kmean.reshape(bp, tp // blk, dm))

    zero_taps = [[jnp.zeros((bp, 2, ff), F32)] for _ in range(depth)]
    y_p, k_p, v_p, gla_p, conv_p = _trunk(
        x_prompt, mods_p, per_batch(kv_mod_all, 2),
        [gla_prompt(i) for i in range(n_a)], zero_taps, None, kvq_prompt,
        lambda ctx: _moba_prompt(*ctx, hd), w, tiles_p)

    n_rows = bs * ts
    xs = x_sample.reshape(1, n_rows, d)
    pos_s = past_len + jnp.arange(ts, dtype=jnp.int32)

    def gla_sample(i):
        s0 = jnp.swapaxes(state_gla[i], -1, -2)
        sh_m, sc_m = [m[bp:bp + bs, j * d:(j + 1) * d][:, None, :] for m in [mod_all[i]]
                      for j in range(2)]

        def run(x):
            xp = _pad_rows(x.reshape(bs, ts, d), GLA_CHUNK)
            o, s_fin = _gla(xp, sh_m, sc_m, *gla_weights(i), s0, norm_g[i], heads, dk, dv,
                            GLA_CHUNK, ts)
            return o[:, :ts].reshape(1, n_rows, heads * dv), s_fin
        return run

    qpad = 8
    cache_k_t = jnp.transpose(cache_k, (0, 2, 3, 1))
    cache_v_t = jnp.transpose(cache_v, (0, 2, 3, 1))
    tables_s = _rope_tables(jnp.tile(pos_s, bs), hd)
    tiles_s = {"row": n_rows, "ffn": n_rows}

    def kvq_sample(x, *mods):
        k, v, q = _kvq_decode(x, mods, w_k, w_v, w_q, tables_s, hd, tiles_s["row"])
        return k, v, (q, k, v)

    def attend_sample(ctx):
        q, k, v = ctx
        seqs = lambda a, n: _pad_rows(a.reshape(bs, ts, dm), n)
        head_mask = (jnp.arange(dm)[None, :] // hd == jnp.arange(m_heads)[:, None]).astype(F32)
        qm = (q.reshape(bs, 1, ts, dm) * head_mask[None, :, None, :]).reshape(bs, m_heads * ts, dm)
        o = _moba_decode(page_table, qm, seqs(k, LANES), seqs(v, LANES), cache_k_t, cache_v_t,
                         ts, qpad)
        return o[:, :ts].reshape(1, n_rows, dm)

    def taps(i):
        st = state_ffn_conv[i]
        zeros = jnp.zeros((bs, ts - 1, ff), F32)
        tap1 = jnp.concatenate([st[:, 1:2], zeros], axis=1)
        tap2 = jnp.concatenate([st[:, 0:1], st[:, 1:2], zeros[:, :ts - 2]], axis=1)
        return [tap1.reshape(1, n_rows, ff), tap2.reshape(1, n_rows, ff)]

    y_s, k_s, v_s, gla_s, conv_s = _trunk(
        xs, [per_token(m, 6) for m in mod_all], per_token(kv_mod_all, 2),
        [gla_sample(i) for i in range(n_a)], [taps(i) for i in range(depth)], ts, kvq_sample,
        attend_sample, w, tiles_s)

    gla_state_p = jnp.stack([jnp.swapaxes(s, -1, -2) for s in gla_p])
    gla_state_s = jnp.stack([jnp.swapaxes(s, -1, -2) for s in gla_s])
    conv_state_p = jnp.stack(conv_p)
    conv_state_s = jnp.stack([g.reshape(bs, ts, ff)[:, ts - 2:] for g in conv_s])
    return (y_p, y_s.reshape(bs, ts, d),
            jnp.transpose(k_p, (0, 3, 1, 2)), jnp.transpose(v_p, (0, 3, 1, 2)),
            k_s.reshape(bs, ts, m_heads, hd), v_s.reshape(bs, ts, m_heads, hd),
            gla_state_p, gla_state_s, conv_state_p, conv_state_s)
```

```python
import functools

import jax
import jax.numpy as jnp
from jax import lax
from jax.experimental import pallas as pl
from jax.experimental.pallas import tpu as pltpu

F32 = jnp.float32
BF = jnp.bfloat16

LN_EPS = 1e-5
GLA_CHUNK = 64
GLA_GATE_NORM = 16.0
GLA_GATE_RANK_PAD = 128
MOBA_BLOCK = 256
MOBA_TOPK = 3
ROPE_THETA = 10000.0
MASKED = -1e30
LOG2_E = 1.4426950408889634
SUM_ROWS = 16
LANES = 128
VMEM_LIMIT = 56 * 1024 * 1024


def _params(*sem):
    return pltpu.CompilerParams(dimension_semantics=sem, vmem_limit_bytes=VMEM_LIMIT)


def _dot(a, b):
    return jnp.dot(a, b, preferred_element_type=F32)


def _dot_nt(a, b):
    return lax.dot_general(a, b, (((1,), (1,)), ((), ())), preferred_element_type=F32)


def _dot_tn(a, b):
    return lax.dot_general(a, b, (((0,), (0,)), ((), ())), preferred_element_type=F32)


def _layer_norm(z, g, b):
    mu = jnp.mean(z, axis=-1, keepdims=True)
    zc = z - mu
    var = jnp.mean(zc * zc, axis=-1, keepdims=True)
    return zc * lax.rsqrt(var + LN_EPS) * g + b


def _const_spec(shape):
    nd = len(shape)
    return pl.BlockSpec(shape, lambda *_: (0,) * nd, pipeline_mode=pl.Buffered(1))


def _mod_spec(arr, tm):
    if arr.shape[1] == 1:
        return pl.BlockSpec((1, 1, arr.shape[2]), lambda b, t: (b, 0, 0))
    return pl.BlockSpec((1, tm, arr.shape[2]), lambda b, t: (b, t, 0))


def _row_spec(tm, width):
    return pl.BlockSpec((1, tm, width), lambda b, t: (b, t, 0))


def _ada_kernel(c_ref, w_ref, b_ref, o_ref):
    c = c_ref[...]
    s = c * jax.nn.sigmoid(c)
    o_ref[...] = _dot(s.astype(BF), w_ref[...].astype(BF)) + b_ref[...]


def _ada(c, w3, layer, b2, tn=1024):
    r, d = c.shape
    n = w3.shape[2]
    return pl.pallas_call(
        _ada_kernel,
        out_shape=jax.ShapeDtypeStruct((r, n), F32),
        grid=(n // tn,),
        in_specs=[pl.BlockSpec((r, d), lambda j: (0, 0)),
                  pl.BlockSpec((None, d, tn), lambda j: (layer, 0, j)),
                  pl.BlockSpec((1, tn), lambda j: (0, j))],
        out_specs=pl.BlockSpec((r, tn), lambda j: (0, j)),
        compiler_params=_params("arbitrary"),
        name="ada_mod",
    )(c, w3, b2)


def _split3(x):
    hi = x.astype(BF)
    r = x - hi.astype(F32)
    mid = r.astype(BF)
    lo = (r - mid.astype(F32)).astype(BF)
    return hi, mid, lo


def _gla_kernel(x_ref, sh_ref, sc_ref, win_ref, wg1_ref, wg2_ref, bgk_ref, s0_ref, ng_ref,
                o_ref, sfin_ref, s_scr, q_scr, k_scr, v_scr, g_scr, gk_scr,
                *, heads, dk, dv, chunk, n_valid, scale):
    t = pl.program_id(1)
    tm = x_ref.shape[1]
    kd, vd = heads * dk, heads * dv

    @pl.when(t == 0)
    def _():
        s_scr[...] = s0_ref[0]

    h = (x_ref[0] * (1.0 + sc_ref[0]) + sh_ref[0]).astype(BF)
    q_scr[...] = _dot(h, win_ref[:, 0:kd]) * scale
    k = _dot(h, win_ref[:, kd:2 * kd])
    v_scr[...] = _dot(h, win_ref[:, 2 * kd:2 * kd + vd]).astype(BF)
    g_scr[...] = _dot(h, win_ref[:, 2 * kd + vd:2 * kd + 2 * vd])
    lin = _dot(_dot(h, wg1_ref[...]).astype(BF), wg2_ref[...]) + bgk_ref[...]
    gk = (jnp.minimum(lin, 0.0) - jnp.log1p(jnp.exp(-jnp.abs(lin)))) * (1.0 / GLA_GATE_NORM)
    if n_valid < tm:
        valid = lax.broadcasted_iota(jnp.int32, k.shape, 0) < n_valid
        k = jnp.where(valid, k, 0.0)
        gk = jnp.where(valid, gk, 0.0)
    k_scr[...] = k
    gk_scr[...] = gk

    row = lax.broadcasted_iota(jnp.int32, (chunk, chunk), 0)
    col = lax.broadcasted_iota(jnp.int32, (chunk, chunk), 1)
    tril_f = (row >= col).astype(F32)
    tril = tril_f.astype(BF)
    ng = ng_ref[...]

    pre = []
    for c in range(tm // chunk):
        rows = pl.ds(c * chunk, chunk)
        hi, mid, lo = _split3(gk_scr[rows, :])
        cum = _dot(tril, hi) + _dot(tril, mid) + _dot(tril, lo)
        last = cum[chunk - 1:chunk, :]
        q_in = (q_scr[rows, :] * jnp.exp(cum)).astype(BF)
        kc = k_scr[rows, :]
        k_in = (kc * jnp.exp(-cum)).astype(BF)
        k_end = (kc * jnp.exp(last - cum)).astype(BF)
        dec = jnp.exp(last)
        per_head = []
        for hh in range(heads):
            ks = slice(hh * dk, (hh + 1) * dk)
            vh = v_scr[rows, hh * dv:(hh + 1) * dv]
            a = _dot_nt(q_in[:, ks], k_in[:, ks]) * tril_f
            per_head.append((q_in[:, ks], _dot(a.astype(BF), vh), _dot_tn(vh, k_end[:, ks]),
                             dec[:, ks]))
        pre.append(per_head)
    for c in range(tm // chunk):
        rows = pl.ds(c * chunk, chunk)
        for hh in range(heads):
            vs = slice(hh * dv, (hh + 1) * dv)
            q_h, o_intra, kv, dec_h = pre[c][hh]
            st = s_scr[hh]
            o = o_intra + _dot_nt(q_h, st.astype(BF))
            s_scr[hh] = st * dec_h + kv
            o = o * lax.rsqrt(jnp.mean(o * o, axis=-1, keepdims=True) + LN_EPS) * ng
            gh = g_scr[rows, vs]
            o_ref[0, rows, vs] = (o * (gh * jax.nn.sigmoid(gh))).astype(BF)

    @pl.when(t == pl.num_programs(1) - 1)
    def _():
        sfin_ref[0] = s_scr[...]


def _gla(x, sh, sc, w_in, wg1, wg2, bgk, s0t, norm_g, heads, dk, dv, tm, n_valid):
    b, t, d = x.shape
    kd, vd = heads * dk, heads * dv
    kern = functools.partial(_gla_kernel, heads=heads, dk=dk, dv=dv, chunk=GLA_CHUNK,
                             n_valid=n_valid, scale=float(dk) ** -0.5)
    st_spec = pl.BlockSpec((1, heads, dv, dk), lambda bb, tt: (bb, 0, 0, 0))
    return pl.pallas_call(
        kern,
        out_shape=[jax.ShapeDtypeStruct((b, t, vd), BF),
                   jax.ShapeDtypeStruct((b, heads, dv, dk), F32)],
        grid=(b, t // tm),
        in_specs=[_row_spec(tm, d), _mod_spec(sh, tm), _mod_spec(sc, tm),
                  _const_spec(w_in.shape), _const_spec(wg1.shape), _const_spec(wg2.shape),
                  _const_spec(bgk.shape), st_spec, _const_spec(norm_g.shape)],
        out_specs=[_row_spec(tm, vd), st_spec],
        scratch_shapes=[pltpu.VMEM((heads, dv, dk), F32), pltpu.VMEM((tm, kd), F32),
                        pltpu.VMEM((tm, kd), F32), pltpu.VMEM((tm, vd), BF),
                        pltpu.VMEM((tm, vd), F32), pltpu.VMEM((tm, kd), F32)],
        compiler_params=_params("parallel", "arbitrary"),
        name="gla",
    )(x, sh, sc, w_in, wg1, wg2, bgk, s0t, norm_g)


def _ffn_kernel(*refs, alpha, ff, fc, tm, seq_len):
    x_ref, o_ref, gm_ref, wout_ref, lng0_ref, lnb0_ref, sh_ref, sc_ref, gate_ref = refs[:9]
    if seq_len is None:
        (st_ref, wup_ref, wconv_ref, bconv_ref, wdown_ref,
         lng_ref, lnb_ref, out_ref, tail_ref, act, carry) = refs[9:]
    else:
        (tap1_ref, tap2_ref, wup_ref, wconv_ref, bconv_ref,
         wdown_ref, lng_ref, lnb_ref, out_ref, tail_ref, act) = refs[9:]
    t = pl.program_id(1)
    x = _layer_norm(alpha * x_ref[0] + (1.0 + gm_ref[0]) * _dot(o_ref[0], wout_ref[...]),
                    lng0_ref[...], lnb0_ref[...])
    h = (x * (1.0 + sc_ref[0]) + sh_ref[0]).astype(BF)
    rowi = lax.broadcasted_iota(jnp.int32, (tm, fc), 0)
    for c in range(ff // fc):
        cs = slice(c * fc, (c + 1) * fc)
        u = _dot(h, wup_ref[:, c * fc:(c + 1) * fc])
        g = _dot(h, wup_ref[:, ff + c * fc:ff + (c + 1) * fc])
        g1 = pltpu.roll(g, 1, 0)
        g2 = pltpu.roll(g, 2, 0)
        if seq_len is None:
            prev = jnp.where(t == 0, st_ref[0, :, cs], carry[0:2, cs])
            g1 = jnp.where(rowi == 0, prev[1:2, :], g1)
            g2 = jnp.where(rowi == 0, prev[0:1, :], jnp.where(rowi == 1, prev[1:2, :], g2))
            carry[0:2, cs] = g[tm - 2:tm, :]
            tail_ref[0, :, cs] = g[tm - 2:tm, :]
        else:
            pos = rowi & (seq_len - 1)
            g1 = jnp.where(pos >= 1, g1, tap1_ref[0, :, cs])
            g2 = jnp.where(pos >= 2, g2, tap2_ref[0, :, cs])
            tail_ref[0, :, cs] = g
        gc = (bconv_ref[:, cs] + g2 * wconv_ref[0:1, cs] + g1 * wconv_ref[1:2, cs]
              + g * wconv_ref[2:3, cs])
        act[:, cs] = (0.5 * gc * (1.0 + lax.erf(gc * 0.7071067811865476)) * u).astype(BF)
    z = alpha * x + (1.0 + gate_ref[0]) * _dot(act[...], wdown_ref[...])
    out_ref[0] = _layer_norm(z, lng_ref[...], lnb_ref[...])


def _ffn(x, o, gate_m, w_out, lng0, lnb0, sh, sc, gate, taps, w_up, w_conv, b_conv, w_down, lng,
         lnb, alpha, tm, seq_len):
    b, t, d = x.shape
    ff = w_down.shape[0]
    fc = 256 if ff % 256 == 0 else LANES
    kern = functools.partial(_ffn_kernel, alpha=alpha, ff=ff, fc=fc, tm=tm, seq_len=seq_len)
    wspecs = [_const_spec(w_up.shape), _const_spec(w_conv.shape), _const_spec(b_conv.shape),
              _const_spec(w_down.shape), _const_spec(lng.shape), _const_spec(lnb.shape)]
    mixer_specs = [_row_spec(tm, d), _row_spec(tm, o.shape[2]), _mod_spec(gate_m, tm),
                   _const_spec(w_out.shape), _const_spec(lng0.shape), _const_spec(lnb0.shape)]
    if seq_len is None:
        tap_specs = [pl.BlockSpec((1, 2, ff), lambda bb, tt: (bb, 0, 0))]
        tail_shape = jax.ShapeDtypeStruct((b, 2, ff), F32)
        tail_spec = pl.BlockSpec((1, 2, ff), lambda bb, tt: (bb, 0, 0))
        scratch = [pltpu.VMEM((tm, ff), BF), pltpu.VMEM((8, ff), F32)]
        sem = ("parallel", "arbitrary")
    else:
        assert seq_len & (seq_len - 1) == 0 and tm % seq_len == 0
        tap_specs = [_row_spec(tm, ff), _row_spec(tm, ff)]
        tail_shape = jax.ShapeDtypeStruct((b, t, ff), F32)
        tail_spec = _row_spec(tm, ff)
        scratch = [pltpu.VMEM((tm, ff), BF)]
        sem = ("parallel", "parallel")
    return pl.pallas_call(
        kern,
        out_shape=[jax.ShapeDtypeStruct((b, t, d), F32), tail_shape],
        grid=(b, t // tm),
        in_specs=mixer_specs + [_mod_spec(sh, tm), _mod_spec(sc, tm), _mod_spec(gate, tm)]
        + tap_specs + wspecs,
        out_specs=[_row_spec(tm, d), tail_spec],
        scratch_shapes=scratch,
        compiler_params=_params(*sem),
        name="conv_ffn",
    )(x, o, gate_m, w_out, lng0, lnb0, sh, sc, gate, *taps, w_up, w_conv, b_conv, w_down, lng, lnb)


def _rope_rows(x, cos, sin, hd):
    half = hd // 2
    lane = lax.broadcasted_iota(jnp.int32, (x.shape[0], LANES), 1)
    first = (lane & (hd - 1)) < half
    out = []
    for s in range(x.shape[1] // LANES):
        xs = x[:, s * LANES:(s + 1) * LANES]
        partner = jnp.where(first, pltpu.roll(xs, LANES - half, 1), pltpu.roll(xs, half, 1))
        out.append(xs * cos + partner * sin)
    return out


def _rope_cols(x3, cos_t, sin_t):
    half = x3.shape[1] // 2
    partner = jnp.concatenate([x3[:, half:, :], x3[:, :half, :]], axis=1)
    return x3 * cos_t[None] + partner * sin_t[None]


def _kvq_decode_kernel(x_ref, shk_ref, sck_ref, shq_ref, scq_ref, wk_ref, wv_ref, wq_ref,
                       cos_ref, sin_ref, k_ref, v_ref, q_ref, *, hd):
    x = x_ref[0]
    hk = (x * (1.0 + sck_ref[0]) + shk_ref[0]).astype(BF)
    hq = (x * (1.0 + scq_ref[0]) + shq_ref[0]).astype(BF)
    cos = cos_ref[...]
    sin = sin_ref[...]
    for s, piece in enumerate(_rope_rows(_dot(hk, wk_ref[...]), cos, sin, hd)):
        k_ref[0, :, s * LANES:(s + 1) * LANES] = piece
    v_ref[0] = _dot(hk, wv_ref[...])
    for s, piece in enumerate(_rope_rows(_dot(hq, wq_ref[...]), cos, sin, hd)):
        q_ref[0, :, s * LANES:(s + 1) * LANES] = piece


def _kvq_prompt_kernel(x_ref, shk_ref, sck_ref, shq_ref, scq_ref, wk_ref, wkt_ref, wvt_ref,
                       wqt_ref, cos_ref, sin_ref, cos_t_ref, sin_t_ref,
                       kt_ref, vt_ref, krow_ref, vtb_ref, qt_ref, km_ref, *, hd, blk):
    x = x_ref[0]
    tm = x.shape[0]
    hk = (x * (1.0 + sck_ref[0]) + shk_ref[0]).astype(BF)
    hq = (x * (1.0 + scq_ref[0]) + shq_ref[0]).astype(BF)
    heads = kt_ref.shape[1]
    cos_t = cos_t_ref[...]
    sin_t = sin_t_ref[...]
    kt_ref[0] = _rope_cols(_dot_nt(wkt_ref[...], hk).reshape(heads, hd, tm), cos_t, sin_t)
    vt = _dot_nt(wvt_ref[...], hk).reshape(heads, hd, tm)
    vt_ref[0] = vt
    vtb_ref[0, :, 0:hd, :] = vt.astype(BF)
    vtb_ref[0, :, hd:, :] = jnp.ones((heads, vtb_ref.shape[2] - hd, tm), BF)
    qt_ref[0] = _rope_cols(_dot_nt(wqt_ref[...], hq).reshape(heads, hd, tm), cos_t,
                           sin_t).reshape(heads * hd, tm)
    for s, piece in enumerate(_rope_rows(_dot(hk, wk_ref[...]), cos_ref[...], sin_ref[...], hd)):
        cols = slice(s * LANES, (s + 1) * LANES)
        krow_ref[0, :, cols] = piece.astype(BF)
        for j in range(tm // blk):
            km_ref[0, 0, j:j + 1, cols] = jnp.mean(piece[j * blk:(j + 1) * blk, :], axis=0,
                                                   keepdims=True)


def _kvq_specs(x, mods, tm):
    return [_row_spec(tm, x.shape[2])] + [_mod_spec(m, tm) for m in mods]


def _kvq_decode(x, mods, wk, wv, wq, tables, hd, tm):
    b, t, d = x.shape
    dm = wq.shape[1]
    cos, sin = tables[:2]
    tab_spec = pl.BlockSpec((tm, LANES), lambda bb, tt: (tt, 0))
    row = jax.ShapeDtypeStruct((b, t, dm), F32)
    return pl.pallas_call(
        functools.partial(_kvq_decode_kernel, hd=hd),
        out_shape=[row, row, row],
        grid=(b, t // tm),
        in_specs=_kvq_specs(x, mods, tm) + [_const_spec(wk.shape), _const_spec(wv.shape),
                                            _const_spec(wq.shape), tab_spec, tab_spec],
        out_specs=[_row_spec(tm, dm)] * 3,
        compiler_params=_params("parallel", "parallel"),
        name="kvq_decode",
    )(x, *mods, wk, wv, wq, cos, sin)


def _kvq_prompt(x, mods, wk, wv, wq, tables, hd, tm):
    b, t, d = x.shape
    dm = wq.shape[1]
    heads = dm // hd
    blk = MOBA_BLOCK
    cos, sin, cos_t, sin_t = tables
    tab_spec = pl.BlockSpec((tm, LANES), lambda bb, tt: (tt, 0))
    tab_t_spec = pl.BlockSpec((hd, tm), lambda bb, tt: (0, tt))
    col_spec = pl.BlockSpec((1, heads, hd, tm), lambda bb, tt: (bb, 0, 0, tt))
    w_spec = _const_spec(wk.shape)
    return pl.pallas_call(
        functools.partial(_kvq_prompt_kernel, hd=hd, blk=blk),
        out_shape=[jax.ShapeDtypeStruct((b, heads, hd, t), F32),
                   jax.ShapeDtypeStruct((b, heads, hd, t), F32),
                   jax.ShapeDtypeStruct((b, t, dm), BF),
                   jax.ShapeDtypeStruct((b, heads, hd + SUM_ROWS, t), BF),
                   jax.ShapeDtypeStruct((b, dm, t), F32),
                   jax.ShapeDtypeStruct((b, t // tm, tm // blk, dm), F32)],
        grid=(b, t // tm),
        in_specs=_kvq_specs(x, mods, tm) + [w_spec, w_spec, w_spec, w_spec, tab_spec, tab_spec,
                                            tab_t_spec, tab_t_spec],
        out_specs=[col_spec, col_spec, _row_spec(tm, dm),
                   pl.BlockSpec((1, heads, hd + SUM_ROWS, tm), lambda bb, tt: (bb, 0, 0, tt)),
                   pl.BlockSpec((1, dm, tm), lambda bb, tt: (bb, 0, tt)),
                   pl.BlockSpec((1, 1, tm // blk, dm), lambda bb, tt: (bb, tt, 0, 0))],
        compiler_params=_params("parallel", "parallel"),
        name="kvq_prompt",
    )(x, *mods, wk, wk.T, wv.T, wq.T, cos, sin, cos_t, sin_t)


def _rope_tables(pos, hd):
    half = hd // 2
    inv = ROPE_THETA ** (-jnp.arange(half, dtype=F32) / half)
    ang = pos.astype(F32)[:, None] * inv[None, :]
    cos, sin = jnp.cos(ang), jnp.sin(ang)
    cos_h = jnp.concatenate([cos, cos], axis=1)
    sin_h = jnp.concatenate([-sin, sin], axis=1)
    reps = LANES // hd
    return jnp.tile(cos_h, (1, reps)), jnp.tile(sin_h, (1, reps)), cos_h.T, sin_h.T


def _select_bias(gate, n_valid, topk, axis):
    bidx = lax.broadcasted_iota(jnp.int32, gate.shape, axis)
    bidx_f = bidx.astype(F32)
    ninf = jnp.float32(-jnp.inf)
    gate = jnp.where(bidx < n_valid, gate, ninf)
    sel = jnp.zeros(gate.shape, jnp.bool_)
    for _ in range(topk):
        mx = jnp.max(gate, axis=axis, keepdims=True)
        first = jnp.min(jnp.where(gate == mx, bidx_f, jnp.float32(gate.shape[axis])), axis=axis,
                        keepdims=True)
        pick = (bidx_f == first) & (mx > ninf)
        sel = sel | pick
        gate = jnp.where(pick, ninf, gate)
    return jnp.where(sel, 0.0, MASKED)


def _gate_scores(q, km):
    return jnp.dot(q, km, preferred_element_type=F32, precision=lax.Precision.HIGHEST)


def _head_rows(qt, hh, hd):
    rowi = lax.broadcasted_iota(jnp.int32, qt.shape, 0)
    return jnp.where((rowi >= hh * hd) & (rowi < (hh + 1) * hd), qt, 0.0)


def _moba_gate_kernel(qt_ref, km_ref, bias_ref, *, hd, blk, topk):
    tq = qt_ref.shape[2]
    km = km_ref[0]
    nb = km.shape[0]
    pos = pl.program_id(2) * tq + lax.broadcasted_iota(jnp.int32, (1, tq), 1)
    own = pos >> (blk.bit_length() - 1)
    lane = lax.broadcasted_iota(jnp.int32, km.shape, 1)
    per = LANES // hd
    km_heads = jnp.concatenate(
        [jnp.where((lane >= hh * hd) & (lane < (hh + 1) * hd), km, 0.0) for hh in range(per)],
        axis=0)
    gate = _gate_scores(km_heads, qt_ref[0])
    for hh in range(per):
        bias_ref[0, hh] = _select_bias(gate[hh * nb:(hh + 1) * nb, :], own, topk, 0)


def _moba_gate(q_t, kmean, hd, tq):
    b, dm, t = q_t.shape
    n_blocks = kmean.shape[1]
    n_heads = LANES // hd
    return pl.pallas_call(
        functools.partial(_moba_gate_kernel, hd=hd, blk=MOBA_BLOCK, topk=MOBA_TOPK),
        out_shape=jax.ShapeDtypeStruct((b, dm // hd, n_blocks, t), F32),
        grid=(b, dm // LANES, t // tq),
        in_specs=[pl.BlockSpec((1, LANES, tq), lambda bb, hp, tt: (bb, hp, tt)),
                  pl.BlockSpec((1, n_blocks, LANES), lambda bb, hp, tt: (bb, 0, hp))],
        out_specs=pl.BlockSpec((1, n_heads, n_blocks, tq), lambda bb, hp, tt: (bb, hp, 0, tt)),
        compiler_params=_params("parallel", "parallel", "parallel"),
        name="moba_gate",
    )(q_t, kmean)


def _moba_prompt_kernel(qt_ref, k_ref, vt_ref, bias_ref, o_ref, qb_scr, acc_scr,
                        *, hd, blk, kb, scale):
    i = pl.program_id(2)
    per = LANES // hd
    n_heads = vt_ref.shape[1]
    heads = range(n_heads)
    key = lax.broadcasted_iota(jnp.int32, (blk, blk), 0)
    qry = lax.broadcasted_iota(jnp.int32, (blk, blk), 1)
    own = pl.multiple_of(i * blk, blk)

    def k_tile(hh, start):
        g = hh // per
        return k_ref[0, pl.ds(start, blk), g * LANES:(g + 1) * LANES]

    for hh in heads:
        g = hh // per
        qh = _head_rows(qt_ref[0, g * LANES:(g + 1) * LANES, :], hh % per, hd)
        qb_scr[hh] = (qh * (scale * LOG2_E)).astype(BF)
    own_s = [jnp.where(key <= qry, _dot(k_tile(hh, own), qb_scr[hh]), MASKED) for hh in heads]
    maxes = [jnp.max(s, axis=0, keepdims=True) for s in own_s]
    own_p = [jnp.exp2(s - m).astype(BF) for s, m in zip(own_s, maxes)]
    for hh in heads:
        acc_scr[hh] = _dot(vt_ref[0, hh, :, pl.ds(own, blk)], own_p[hh])

    def body(it, carry):
        tiles = [(hh, b, pl.multiple_of((it * kb + b) * blk, blk))
                 for b in range(kb) for hh in heads]
        scores = [_dot(k_tile(hh, start), qb_scr[hh]) for hh, b, start in tiles]
        parts = [[] for _ in heads]
        probs = []
        for (hh, b, start), s in zip(tiles, scores):
            cm = jnp.max(s, axis=0, keepdims=True)
            probs.append(jnp.exp2(s - cm).astype(BF))
            parts[hh].append([cm + bias_ref[0, hh, pl.ds(it * kb + b, 1), :]])
        for (hh, b, start), p in zip(tiles, probs):
            parts[hh][b].append(_dot(vt_ref[0, hh, :, pl.ds(start, blk)], p))
        new = []
        for hh in heads:
            m_prev = carry[hh]
            m_new = m_prev
            for m_b, _ in parts[hh]:
                m_new = jnp.maximum(m_new, m_b)
            acc = jnp.exp2(m_prev - m_new) * acc_scr[hh]
            for m_b, acc_b in parts[hh]:
                acc = acc + jnp.exp2(m_b - m_new) * acc_b
            acc_scr[hh] = acc
            new.append(m_new)
        return tuple(new)

    lax.fori_loop(0, (i + kb - 1) // kb, body, tuple(maxes))
    for g in range(n_heads // per):
        out_t = jnp.concatenate([acc_scr[hh, 0:hd, :] / acc_scr[hh, hd:hd + 1, :]
                                 for hh in range(g * per, (g + 1) * per)], axis=0)
        o_ref[0, :, g * LANES:(g + 1) * LANES] = out_t.T.astype(o_ref.dtype)


def _moba_prompt(q_t, k, v_t, kmean, hd):
    b, dm, t = q_t.shape
    blk = MOBA_BLOCK
    n_blocks = t // blk
    kb = 2 if n_blocks % 2 == 0 else 1
    width = 4 * LANES if dm % (4 * LANES) == 0 else LANES
    n_heads = width // hd
    bias = _moba_gate(q_t, kmean, hd, min(t, 2048))
    kern = functools.partial(_moba_prompt_kernel, hd=hd, blk=blk, kb=kb, scale=float(hd) ** -0.5)
    return pl.pallas_call(
        kern,
        out_shape=jax.ShapeDtypeStruct((b, t, dm), BF),
        grid=(b, dm // width, n_blocks),
        in_specs=[pl.BlockSpec((1, width, blk), lambda bb, hg, i: (bb, hg, i)),
                  pl.BlockSpec((1, t, width), lambda bb, hg, i: (bb, 0, hg)),
                  pl.BlockSpec((1, n_heads, v_t.shape[2], t), lambda bb, hg, i: (bb, hg, 0, 0)),
                  pl.BlockSpec((1, n_heads, n_blocks, blk), lambda bb, hg, i: (bb, hg, 0, i))],
        out_specs=pl.BlockSpec((1, blk, width), lambda bb, hg, i: (bb, i, hg)),
        scratch_shapes=[pltpu.VMEM((n_heads, LANES, blk), BF),
                        pltpu.VMEM((n_heads, v_t.shape[2], blk), F32)],
        compiler_params=_params("parallel", "parallel", "arbitrary"),
        name="moba_prompt",
    )(q_t, k, v_t, bias)


def _moba_decode_kernel(*refs, hd, ts, bps, n_new, n_blocks, topk, scale):
    n_pg = 2 * bps
    pt_ref, qm_ref, kn_ref, vn_ref = refs[:4]
    k_refs = refs[4:4 + n_pg]
    v_refs = refs[4 + n_pg:4 + 2 * n_pg]
    o_ref, qb_scr, km_scr, m_scr, l_scr, acc_all = refs[4 + 2 * n_pg:]
    j = pl.program_id(1)
    rows, dm = qm_ref.shape[1], qm_ref.shape[2]
    page = k_refs[0].shape[3]
    blk = 2 * page
    col_km = lax.broadcasted_iota(jnp.int32, km_scr.shape, 1)
    col_st = lax.broadcasted_iota(jnp.int32, m_scr.shape, 1)
    row_h = lax.broadcasted_iota(jnp.int32, (rows, dm), 0) >> (ts.bit_length() - 1)
    lane_h = lax.broadcasted_iota(jnp.int32, (rows, dm), 1) >> (hd.bit_length() - 1)

    def own_head_lanes(x):
        x = jnp.where(row_h == lane_h, x, 0.0)
        out = x[:, 0:LANES]
        for g in range(1, dm // LANES):
            out = out + x[:, g * LANES:(g + 1) * LANES]
        return out

    @pl.when(j == 0)
    def _():
        qb_scr[...] = (qm_ref[0] * (scale * LOG2_E)).astype(BF)
        km_scr[...] = jnp.zeros_like(km_scr)
        m_scr[...] = jnp.full(m_scr.shape, MASKED, F32)
        l_scr[...] = jnp.zeros_like(l_scr)

    qb = qb_scr[...]
    for b in range(bps):
        idx = j * bps + b
        ke = k_refs[2 * b][0].reshape(dm, page)
        ko = k_refs[2 * b + 1][0].reshape(dm, page)
        mean = jnp.sum(ke + ko, axis=-1, keepdims=True) * (1.0 / blk)
        km_scr[...] = jnp.where(col_km == idx, mean, km_scr[...])
        s = jnp.concatenate([_dot(qb, ke.astype(BF)), _dot(qb, ko.astype(BF))], axis=-1)
        cm = jnp.max(s, axis=-1, keepdims=True)
        p = jnp.exp2(s - cm)
        m_scr[...] = jnp.where(col_st == idx, cm, m_scr[...])
        l_scr[...] = jnp.where(col_st == idx, jnp.sum(p, axis=-1, keepdims=True), l_scr[...])
        pb = p.astype(BF)
        acc_all[idx] = own_head_lanes(
            _dot_nt(pb[:, :page], v_refs[2 * b][0].reshape(dm, page).astype(BF))
            + _dot_nt(pb[:, page:], v_refs[2 * b + 1][0].reshape(dm, page).astype(BF)))

    @pl.when(j == pl.num_programs(1) - 1)
    def _():
        gate = _gate_scores(qm_ref[0], km_scr[...])
        m_blk = m_scr[...] + _select_bias(gate, n_blocks, topk, 1)
        s = _dot_nt(qb, kn_ref[0].astype(BF))
        qpos = lax.broadcasted_iota(jnp.int32, s.shape, 0) & (ts - 1)
        kpos = lax.broadcasted_iota(jnp.int32, s.shape, 1)
        s = jnp.where((kpos <= qpos) & (kpos < n_new), s, MASKED)
        m_own = jnp.max(s, axis=-1, keepdims=True)
        p = jnp.exp2(s - m_own)
        m_all = jnp.maximum(m_own, jnp.max(m_blk, axis=-1, keepdims=True))
        w_own = jnp.exp2(m_own - m_all)
        w_blk = jnp.exp2(m_blk - m_all)
        l_all = (w_own * jnp.sum(p, axis=-1, keepdims=True)
                 + jnp.sum(w_blk * l_scr[...], axis=-1, keepdims=True))
        acc0 = w_own * own_head_lanes(_dot(p.astype(BF), vn_ref[0].astype(BF)))

        accs = [acc0] + [jnp.zeros_like(acc0)] * 3
        for jb in range(n_blocks):
            accs[jb % 4] = accs[jb % 4] + w_blk[:, jb:jb + 1] * acc_all[jb]
        res = ((accs[0] + accs[1] + (accs[2] + accs[3])) / l_all).astype(BF)
        qrow = lax.broadcasted_iota(jnp.int32, (o_ref.shape[1], rows), 0)
        r = lax.broadcasted_iota(jnp.int32, (o_ref.shape[1], rows), 1)
        r_group = r >> ((ts * (LANES // hd)).bit_length() - 1)
        for g in range(dm // LANES):
            pick = ((r & (ts - 1)) == qrow) & (r_group == g)
            o_ref[0, :, g * LANES:(g + 1) * LANES] = _dot(pick.astype(BF), res).astype(o_ref.dtype)


def _moba_decode(page_table, qm, k_new, v_new, cache_k_t, cache_v_t, ts, qpad):
    s, rows, dm = qm.shape
    _, n_heads, hd, page = cache_k_t.shape
    n_blocks = page_table.shape[1] // 2
    bps = 8 if n_blocks % 8 == 0 else 1
    assert 2 * page == MOBA_BLOCK and n_blocks <= LANES and rows == n_heads * ts
    kern = functools.partial(_moba_decode_kernel, hd=hd, ts=ts, bps=bps, n_new=ts,
                             n_blocks=n_blocks, topk=MOBA_TOPK, scale=float(hd) ** -0.5)

    def page_spec(k):
        return pl.BlockSpec((1, n_heads, hd, page),
                            lambda ss, j, pt: (pt[ss, 2 * bps * j + k], 0, 0, 0))

    pages = [page_spec(k) for k in range(2 * bps)]
    per_seq = lambda a: pl.BlockSpec((1,) + a.shape[1:], lambda ss, j, pt: (ss, 0, 0))
    grid_spec = pltpu.PrefetchScalarGridSpec(
        num_scalar_prefetch=1,
        grid=(s, n_blocks // bps),
        in_specs=[per_seq(qm), per_seq(k_new), per_seq(v_new)] + pages + pages,
        out_specs=pl.BlockSpec((1, qpad, dm), lambda ss, j, pt: (ss, 0, 0)),
        scratch_shapes=[pltpu.VMEM((rows, dm), BF),
                        pltpu.VMEM((dm, LANES), F32),
                        pltpu.VMEM((rows, LANES), F32),
                        pltpu.VMEM((rows, LANES), F32),
                        pltpu.VMEM((n_blocks, rows, LANES), F32)],
    )
    return pl.pallas_call(
        kern,
        out_shape=jax.ShapeDtypeStruct((s, qpad, dm), BF),
        grid_spec=grid_spec,
        compiler_params=_params("parallel", "arbitrary"),
        name="moba_decode",
    )(page_table, qm, k_new, v_new, *([cache_k_t] * (2 * bps)), *([cache_v_t] * (2 * bps)))


def _pad_rows(a, n, axis=1):
    pad = [(0, 0)] * a.ndim
    pad[axis] = (0, n - a.shape[axis])
    return jnp.pad(a, pad)


def _trunk(x, mods, kv_mod, gla_fns, conv_taps, seq_len, kvq_fn, attend, w, tiles):
    depth = len(mods)
    n_a = depth // 2
    alpha = float((2 * depth) ** 0.25)
    gla_states, conv_tails = [], []
    k_sh = v_sh = ctx = None
    for i in range(depth):
        sh_m, sc_m, g_m, sh_f, sc_f, g_f = mods[i]
        if i < n_a:
            o, s_fin = gla_fns[i](x)
            gla_states.append(s_fin)
            w_out = w["a_w_out"][i]
        else:
            o = attend(ctx)
            w_out = w["b_w_out"][i - n_a]
        x, tail = _ffn(x, o, g_m, w_out, w["ln_g"][i, 0:1], w["ln_b"][i, 0:1], sh_f, sc_f, g_f,
                       conv_taps[i], w["w_up"][i], w["w_conv"][i], w["b_conv"][i:i + 1],
                       w["w_down"][i], w["ln_g"][i, 1:2], w["ln_b"][i, 1:2], alpha, tiles["ffn"],
                       seq_len)
        conv_tails.append(tail)
        if i == n_a - 1:
            k_sh, v_sh, ctx = kvq_fn(x, kv_mod[0], kv_mod[1], mods[n_a][0], mods[n_a][1])
    return x, k_sh, v_sh, gla_states, conv_tails


def kernel(x_prompt, x_sample, cache_k, cache_v, state_gla, state_ffn_conv, page_table, c_prompt,
           c_sample, w_ada, b_ada, ln_g, ln_b, w_up, w_conv, b_conv, w_down, a_w_in, a_w_gk1,
           a_w_gk2, a_b_gk, a_norm_g, a_w_out, kv_w_ada, kv_b_ada, kv_w, b_w_q, b_w_out):
    bp, tp, d = x_prompt.shape
    bs, ts, _ = x_sample.shape
    depth = w_ada.shape[0]
    n_a = state_gla.shape[0]
    heads, dk, dv = state_gla.shape[2:]
    n_phys, page, m_heads, hd = cache_k.shape
    dm = m_heads * hd
    n_pages = page_table.shape[1]
    past_len = n_pages * page
    ff = w_down.shape[1]
    blk = MOBA_BLOCK
    assert depth == 2 * n_a == 2 and b_w_q.shape[0] == 1
    assert tp % blk == 0 and past_len % blk == 0
    assert 2 <= ts <= 8 and ts & (ts - 1) == 0
    assert LANES % hd == 0 and dm % LANES == 0

    w_k, w_v = kv_w[:, :dm].astype(BF), kv_w[:, dm:].astype(BF)
    w_q = b_w_q[0].astype(BF)
    per_layer = lambda a: [a[i].astype(BF) for i in range(a.shape[0])]
    w = {
        "a_w_in": per_layer(a_w_in),
        "a_w_gk1": _pad_rows(a_w_gk1, GLA_GATE_RANK_PAD, axis=2).astype(BF),
        "a_w_gk2": _pad_rows(a_w_gk2, GLA_GATE_RANK_PAD, axis=1).astype(BF),
        "a_b_gk": a_b_gk[:, None, :], "a_w_out": per_layer(a_w_out),
        "w_up": per_layer(w_up), "w_conv": w_conv, "b_conv": b_conv, "w_down": per_layer(w_down),
        "ln_g": ln_g, "ln_b": ln_b, "b_w_out": per_layer(b_w_out),
    }
    norm_g = [a_norm_g[i][None, :] for i in range(n_a)]

    n_c = bp + bs
    c_all = _pad_rows(jnp.concatenate([c_prompt, c_sample], axis=0), -(-n_c // 8) * 8, axis=0)
    mod_all = [_ada(c_all, w_ada, i, b_ada[i:i + 1]) for i in range(depth)]
    kv_mod_all = _ada(c_all, kv_w_ada[None], 0, kv_b_ada[None, :])

    def per_batch(m, n):
        return [m[:bp, j * d:(j + 1) * d][:, None, :] for j in range(n)]

    def per_token(m, n):
        return [jnp.repeat(m[bp:bp + bs, j * d:(j + 1) * d], ts, axis=0)[None] for j in range(n)]

    def gla_weights(i):
        return (w["a_w_in"][i], w["a_w_gk1"][i], w["a_w_gk2"][i], w["a_b_gk"][i])

    mods_p = [per_batch(m, 6) for m in mod_all]

    def gla_prompt(i):
        s0 = jnp.zeros((bp, heads, dv, dk), F32)
        return lambda x: _gla(x, mods_p[i][0], mods_p[i][1], *gla_weights(i), s0, norm_g[i],
                              heads, dk, dv, tiles_p["gla"], tiles_p["gla"])

    tables_p = _rope_tables(jnp.arange(tp, dtype=jnp.int32), hd)
    tiles_p = {"gla": 512, "row": 512, "ffn": 512}

    def kvq_prompt(x, *mods):
        k_t, v_t, k_row, v_tb, q_t, kmean = _kvq_prompt(x, mods, w_k, w_v, w_q, tables_p, hd,
                                                        tiles_p["row"])
        return k_t, v_t, (q_t, k_row, v_tb, kmean.reshape(bp, tp // blk, dm))

    zero_taps = [[jnp.zeros((bp, 2, ff), F32)] for _ in range(depth)]
    y_p, k_p, v_p, gla_p, conv_p = _trunk(
        x_prompt, mods_p, per_batch(kv_mod_all, 2),
        [gla_prompt(i) for i in range(n_a)], zero_taps, None, kvq_prompt,
        lambda ctx: _moba_prompt(*ctx, hd), w, tiles_p)

    n_rows = bs * ts
    xs = x_sample.reshape(1, n_rows, d)
    pos_s = past_len + jnp.arange(ts, dtype=jnp.int32)

    def gla_sample(i):
        s0 = jnp.swapaxes(state_gla[i], -1, -2)
        sh_m, sc_m = [m[bp:bp + bs, j * d:(j + 1) * d][:, None, :] for m in [mod_all[i]]
                      for j in range(2)]

        def run(x):
            xp = _pad_rows(x.reshape(bs, ts, d), GLA_CHUNK)
            o, s_fin = _gla(xp, sh_m, sc_m, *gla_weights(i), s0, norm_g[i], heads, dk, dv,
                            GLA_CHUNK, ts)
            return o[:, :ts].reshape(1, n_rows, heads * dv), s_fin
        return run

    qpad = 8
    cache_k_t = jnp.transpose(cache_k, (0, 2, 3, 1))
    cache_v_t = jnp.transpose(cache_v, (0, 2, 3, 1))
    tables_s = _rope_tables(jnp.tile(pos_s, bs), hd)
    tiles_s = {"row": n_rows, "ffn": n_rows}

    def kvq_sample(x, *mods):
        k, v, q = _kvq_decode(x, mods, w_k, w_v, w_q, tables_s, hd, tiles_s["row"])
        return k, v, (q, k, v)

    def attend_sample(ctx):
        q, k, v = ctx
        seqs = lambda a, n: _pad_rows(a.reshape(bs, ts, dm), n)
        head_mask = (jnp.arange(dm)[None, :] // hd == jnp.arange(m_heads)[:, None]).astype(F32)
        qm = (q.reshape(bs, 1, ts, dm) * head_mask[None, :, None, :]).reshape(bs, m_heads * ts, dm)
        o = _moba_decode(page_table, qm, seqs(k, LANES), seqs(v, LANES), cache_k_t, cache_v_t,
                         ts, qpad)
        return o[:, :ts].reshape(1, n_rows, dm)

    def taps(i):
        st = state_ffn_conv[i]
        zeros = jnp.zeros((bs, ts - 1, ff), F32)
        tap1 = jnp.concatenate([st[:, 1:2], zeros], axis=1)
        tap2 = jnp.concatenate([st[:, 0:1], st[:, 1:2], zeros[:, :ts - 2]], axis=1)
        return [tap1.reshape(1, n_rows, ff), tap2.reshape(1, n_rows, ff)]

    y_s, k_s, v_s, gla_s, conv_s = _trunk(
        xs, [per_token(m, 6) for m in mod_all], per_token(kv_mod_all, 2),
        [gla_sample(i) for i in range(n_a)], [taps(i) for i in range(depth)], ts, kvq_sample,
        attend_sample, w, tiles_s)

    gla_state_p = jnp.stack([jnp.swapaxes(s, -1, -2) for s in gla_p])
    gla_state_s = jnp.stack([jnp.swapaxes(s, -1, -2) for s in gla_s])
    conv_state_p = jnp.stack(conv_p)
    conv_state_s = jnp.stack([g.reshape(bs, ts, ff)[:, ts - 2:] for g in conv_s])
    return (y_p, y_s.reshape(bs, ts, d),
            jnp.transpose(k_p, (0, 3, 1, 2)), jnp.transpose(v_p, (0, 3, 1, 2)),
            k_s.reshape(bs, ts, m_heads, hd), v_s.reshape(bs, ts, m_heads, hd),
            gla_state_p, gla_state_s, conv_state_p, conv_state_s)
```
